```python
import jax, jax.numpy as jnp
from jax import lax
import numpy as np

D_MODEL = 1024
BATCH = 32
SEQ = 2048
DEPTH = 4

N_MIXERS = 2
N_ATTN_LAYERS = (DEPTH + 1) // 2
N_RNN_LAYERS = DEPTH // 2
ATTN_HEADS = 16
ATTN_HEAD_DIM = D_MODEL // ATTN_HEADS
Q_BLOCK = 128
RNN_WIDTH = 1280
RNN_HEADS = 16
RNN_BLOCK = RNN_WIDTH // RNN_HEADS
RNN_CONV = 4
LRU_C = 8.0
FFN_DIM = 2816
FFN_CONV = 3
PLE_DIM = 256
EPS = 1e-6

kernel_name = "hybrid_stickbreak_rglru_convffn"


def rms_norm(x, g):
    xf = x.astype(jnp.float32)
    y = xf * lax.rsqrt(jnp.mean(xf * xf, axis=-1, keepdims=True) + EPS)
    return (y * g.astype(jnp.float32)).astype(x.dtype)


def causal_depthwise_conv(x, w, b):
    k = w.shape[0]
    c = x.shape[-1]
    y = lax.conv_general_dilated(
        x, w[:, None, :].astype(x.dtype), window_strides=(1,), padding=[(k - 1, 0)],
        dimension_numbers=("NWC", "WIO", "NWC"), feature_group_count=c)
    return y + b.astype(x.dtype)


def stick_breaking_attention(h, w_qkv, w_o):
    b, s, _ = h.shape
    q, k, v = jnp.split(h @ w_qkv, 3, axis=-1)

    def heads(t):
        return t.reshape(b, s, ATTN_HEADS, ATTN_HEAD_DIM).transpose(0, 2, 1, 3).astype(jnp.float32)

    q, k, v = heads(q), heads(k), heads(v)
    scale = ATTN_HEAD_DIM ** -0.5
    outs = []
    for start in range(0, s, Q_BLOCK):
        end = start + Q_BLOCK
        qb = q[:, :, start:end]
        kb = k[:, :, :end]
        vb = v[:, :, :end]
        z = jnp.einsum("bhqd,bhkd->bhqk", qb, kb) * scale
        t_idx = jnp.arange(start, end)[:, None]
        s_idx = jnp.arange(end)[None, :]
        causal = s_idx < t_idx
        log_keep = jnp.where(causal, jax.nn.log_sigmoid(-z), 0.0)
        rest = lax.cumsum(log_keep, axis=3, reverse=True) - log_keep
        weights = jnp.where(causal, jnp.exp(jax.nn.log_sigmoid(z) + rest), 0.0)
        outs.append(jnp.einsum("bhqk,bhkd->bhqd", weights, vb))
    o = jnp.concatenate(outs, axis=2)
    o = o.transpose(0, 2, 1, 3).reshape(b, s, D_MODEL).astype(h.dtype)
    return o @ w_o


def rglru_block(h, w_in, conv_w, conv_b, w_gate_a, b_gate_a, w_gate_x, b_gate_x, lru_param, w_out):
    gate_branch, rec_branch = jnp.split(h @ w_in, 2, axis=-1)
    xr = causal_depthwise_conv(rec_branch, conv_w, conv_b)
    b, s, _ = xr.shape
    xb = xr.reshape(b, s, RNN_HEADS, RNN_BLOCK)
    r = jax.nn.sigmoid(jnp.einsum("bshi,hij->bshj", xb, w_gate_a).reshape(b, s, RNN_WIDTH) + b_gate_a)
    i = jax.nn.sigmoid(jnp.einsum("bshi,hij->bshj", xb, w_gate_x).reshape(b, s, RNN_WIDTH) + b_gate_x)
    log_a = LRU_C * r.astype(jnp.float32) * jax.nn.log_sigmoid(lru_param.astype(jnp.float32))
    a = jnp.exp(log_a)
    mult = jnp.sqrt(-jnp.expm1(2.0 * log_a))
    u = mult * (i * xr).astype(jnp.float32)

    def combine(c1, c2):
        a1, b1 = c1
        a2, b2 = c2
        return a1 * a2, a2 * b1 + b2

    _, hseq = lax.associative_scan(combine, (a, u), axis=1)
    y = jax.nn.gelu(gate_branch) * hseq.astype(h.dtype)
    return y @ w_out


def conv_ffn(h, w_up, conv_w, conv_b, w_down):
    u = causal_depthwise_conv(h @ w_up, conv_w, conv_b)
    gate, val = jnp.split(u, 2, axis=-1)
    return (jax.nn.gelu(gate) * val) @ w_down


def per_layer_embedding(h, p_i, norm_g, w_gate, w_proj):
    g = jax.nn.sigmoid(rms_norm(h, norm_g) @ w_gate)
    return g * (p_i @ w_proj)


def setup_inputs(seed: int = 0) -> dict:
    key = jax.random.key(seed)
    ks = jax.random.split(key, 32)
    f32 = jnp.float32

    def nrm(k, shape, fan_in):
        return jax.random.normal(k, shape, f32) * (fan_in ** -0.5)

    def gain(k, shape):
        return 1.0 + 0.02 * jax.random.normal(k, shape, f32)

    def bias(k, shape):
        return 0.01 * jax.random.normal(k, shape, f32)

    a_base = jax.random.uniform(ks[12], (N_RNN_LAYERS, RNN_WIDTH), f32, minval=0.9, maxval=0.999)
    lru_param = jnp.log(a_base) - jnp.log1p(-a_base)

    return {
        "x": jax.random.normal(ks[0], (BATCH, SEQ, D_MODEL), f32),
        "p": jax.random.normal(ks[1], (DEPTH, BATCH, SEQ, PLE_DIM), f32),
        "norm_mix": gain(ks[2], (DEPTH, D_MODEL)),
        "attn_w_qkv": nrm(ks[3], (N_ATTN_LAYERS, D_MODEL, 3 * D_MODEL), D_MODEL),
        "attn_w_o": nrm(ks[4], (N_ATTN_LAYERS, D_MODEL, D_MODEL), D_MODEL),
        "rnn_w_in": nrm(ks[5], (N_RNN_LAYERS, D_MODEL, 2 * RNN_WIDTH), D_MODEL),
        "rnn_conv_w": nrm(ks[6], (N_RNN_LAYERS, RNN_CONV, RNN_WIDTH), RNN_CONV),
        "rnn_conv_b": bias(ks[7], (N_RNN_LAYERS, RNN_WIDTH)),
        "rnn_w_gate_a": nrm(ks[8], (N_RNN_LAYERS, RNN_HEADS, RNN_BLOCK, RNN_BLOCK), RNN_BLOCK),
        "rnn_b_gate_a": bias(ks[9], (N_RNN_LAYERS, RNN_WIDTH)),
        "rnn_w_gate_x": nrm(ks[10], (N_RNN_LAYERS, RNN_HEADS, RNN_BLOCK, RNN_BLOCK), RNN_BLOCK),
        "rnn_b_gate_x": bias(ks[11], (N_RNN_LAYERS, RNN_WIDTH)),
        "rnn_lru_param": lru_param,
        "rnn_w_out": nrm(ks[13], (N_RNN_LAYERS, RNN_WIDTH, D_MODEL), RNN_WIDTH),
        "norm_ffn": gain(ks[14], (DEPTH, D_MODEL)),
        "ffn_w_up": nrm(ks[15], (DEPTH, D_MODEL, 2 * FFN_DIM), D_MODEL),
        "ffn_conv_w": nrm(ks[16], (DEPTH, FFN_CONV, 2 * FFN_DIM), FFN_CONV),
        "ffn_conv_b": bias(ks[17], (DEPTH, 2 * FFN_DIM)),
        "ffn_w_down": nrm(ks[18], (DEPTH, FFN_DIM, D_MODEL), FFN_DIM),
        "norm_ple": gain(ks[19], (DEPTH, D_MODEL)),
        "ple_w_gate": nrm(ks[20], (DEPTH, D_MODEL, D_MODEL), D_MODEL),
        "ple_w_proj": nrm(ks[21], (DEPTH, PLE_DIM, D_MODEL), PLE_DIM),
        "norm_final": gain(ks[22], (D_MODEL,)),
    }


def reference(x, p, norm_mix, attn_w_qkv, attn_w_o, rnn_w_in, rnn_conv_w, rnn_conv_b,
              rnn_w_gate_a, rnn_b_gate_a, rnn_w_gate_x, rnn_b_gate_x, rnn_lru_param, rnn_w_out,
              norm_ffn, ffn_w_up, ffn_conv_w, ffn_conv_b, ffn_w_down,
              norm_ple, ple_w_gate, ple_w_proj, norm_final):
    for i in range(DEPTH):
        slot = i // N_MIXERS
        hn = rms_norm(x, norm_mix[i])
        if i % N_MIXERS == 0:
            x = x + stick_breaking_attention(hn, attn_w_qkv[slot], attn_w_o[slot])
        else:
            x = x + rglru_block(hn, rnn_w_in[slot], rnn_conv_w[slot], rnn_conv_b[slot],
                                rnn_w_gate_a[slot], rnn_b_gate_a[slot],
                                rnn_w_gate_x[slot], rnn_b_gate_x[slot],
                                rnn_lru_param[slot], rnn_w_out[slot])
        x = x + conv_ffn(rms_norm(x, norm_ffn[i]), ffn_w_up[i], ffn_conv_w[i], ffn_conv_b[i], ffn_w_down[i])
        x = x + per_layer_embedding(x, p[i], norm_ple[i], ple_w_gate[i], ple_w_proj[i])
    return rms_norm(x, norm_final)
```

```python
import functools

import jax
import jax.numpy as jnp
from jax import lax
from jax.experimental import pallas as pl
from jax.experimental.pallas import tpu as pltpu

F32 = jnp.float32
BF16 = jnp.bfloat16

EPS = 1e-6
ATTN_HEADS = 16
RNN_HEADS = 16
LRU_C = 8.0

LANES = 128
SUBLANES = 8
MXU_DIM = 256
VMEM_LIMIT_BYTES = 56 * 1024 * 1024

ROW_TILE = 512
RNN_TILE = 256
ATTN_TILE = MXU_DIM
FFN_CHUNK = MXU_DIM


def _rms_norm(x, g):
    ms = jnp.mean(x * x, axis=-1, keepdims=True)
    return (x * lax.rsqrt(ms + EPS)) * g


def _gelu(x):
    inner = 0.7978845608028654 * (x + 0.044715 * (x * x * x))
    return 0.5 * x * (1.0 + jnp.tanh(inner))


def _sigmoid(x):
    return 1.0 / (1.0 + jnp.exp(-x))


def _dot(a, b):
    return jnp.dot(a, b, preferred_element_type=F32)


def _causal_shift(cur, prev_tail, d):
    sh = pltpu.roll(cur, d, axis=0)
    top_prev = pltpu.roll(prev_tail, d, axis=0)
    row8 = lax.broadcasted_iota(jnp.int32, prev_tail.shape, 0)
    top = jnp.where(row8 < d, top_prev, sh[:SUBLANES])
    return jnp.concatenate([top, sh[SUBLANES:]], axis=0)


def _causal_conv(cur, prev_tail, w, b):
    k = w.shape[0]
    out = cur * w[k - 1:k, :] + b
    for d in range(1, k):
        out = out + _causal_shift(cur, prev_tail, d) * w[k - 1 - d:k - d, :]
    return out


def _qkv_kernel(x_ref, g_ref, w_ref, o_ref, *, d_model, q_scale):
    h = _rms_norm(x_ref[0], g_ref[...]).astype(BF16)
    for c in range(3):
        acc = _dot(h, w_ref[:, c * d_model:(c + 1) * d_model])
        if c == 0:
            acc = acc * q_scale
        o_ref[0, :, c * d_model:(c + 1) * d_model] = acc.astype(BF16)


def _qkv_proj(x, g, w):
    b, s, d = x.shape
    n = w.shape[1]
    tm = min(ROW_TILE, s)
    q_scale = (d // ATTN_HEADS) ** -0.5
    return pl.pallas_call(
        functools.partial(_qkv_kernel, d_model=d, q_scale=q_scale),
        grid=(b, s // tm),
        in_specs=[
            pl.BlockSpec((1, tm, d), lambda i, j: (i, j, 0)),
            pl.BlockSpec((1, d), lambda i, j: (0, 0)),
            pl.BlockSpec((d, n), lambda i, j: (0, 0), pipeline_mode=pl.Buffered(1)),
        ],
        out_specs=pl.BlockSpec((1, tm, n), lambda i, j: (i, j, 0)),
        out_shape=jax.ShapeDtypeStruct((b, s, n), BF16),
        compiler_params=pltpu.CompilerParams(
            dimension_semantics=("parallel", "parallel"), vmem_limit_bytes=VMEM_LIMIT_BYTES),
        name="qkv_proj",
    )(x, g, w)


def _attn_kernel(q_ref, k_ref, v_ref, o_ref, acc_ref, car_ref, *, head_dim):
    qi = pl.program_id(2)
    t = q_ref.shape[1]
    q = q_ref[0]
    lane = lax.broadcasted_iota(jnp.int32, (t, LANES), 1)
    zero = jnp.zeros_like(q)
    q_heads = (jnp.where(lane < head_dim, q, zero), jnp.where(lane >= head_dim, q, zero))
    row = lax.broadcasted_iota(jnp.int32, (t, t), 0)
    col = lax.broadcasted_iota(jnp.int32, (t, t), 1)
    suffix = jnp.where(row >= col, 1.0, 0.0).astype(BF16)
    causal = col < row
    acc_ref[...] = jnp.zeros_like(acc_ref)
    car_ref[...] = jnp.zeros_like(car_ref)

    def block(j, masked):
        start = pl.multiple_of(j * t, t)
        k = k_ref[0, pl.ds(start, t), :]
        v = v_ref[0, pl.ds(start, t), :]
        for h in range(2):
            z = lax.dot_general(q_heads[h], k, (((1,), (1,)), ((), ())), preferred_element_type=F32)
            log_keep = -(jnp.maximum(z, 0.0) + jnp.log(1.0 + jnp.exp(-jnp.abs(z))))
            if masked:
                log_keep = jnp.where(causal, log_keep, 0.0)
            hi = log_keep.astype(BF16)
            lo = (log_keep - hi.astype(F32)).astype(BF16)
            csum = _dot(hi, suffix) + _dot(lo, suffix)
            car = car_ref[h]
            w = jnp.exp(z + csum + jnp.concatenate([car] * (t // LANES), axis=1))
            if masked:
                w = jnp.where(causal, w, 0.0)
            acc_ref[h] += _dot(w.astype(BF16), v)
            car_ref[h] = car + jnp.broadcast_to(csum[:, 0:1], (t, LANES))

    block(qi, True)

    def body(it, carry):
        block(qi - 1 - it, False)
        return carry

    lax.fori_loop(0, qi, body, 0)
    o_ref[0] = jnp.where(lane < head_dim, acc_ref[0], acc_ref[1]).astype(BF16)


def _attention(qkv, d_model):
    b, s, _ = qkv.shape
    head_dim = d_model // ATTN_HEADS
    assert 2 * head_dim == LANES
    t = min(ATTN_TILE, s)
    pairs = d_model // LANES
    return pl.pallas_call(
        functools.partial(_attn_kernel, head_dim=head_dim),
        grid=(b, pairs, s // t),
        in_specs=[
            pl.BlockSpec((1, t, LANES), lambda i, p, j: (i, j, p)),
            pl.BlockSpec((1, s, LANES), lambda i, p, j: (i, 0, pairs + p)),
            pl.BlockSpec((1, s, LANES), lambda i, p, j: (i, 0, 2 * pairs + p)),
        ],
        out_specs=pl.BlockSpec((1, t, LANES), lambda i, p, j: (i, j, p)),
        out_shape=jax.ShapeDtypeStruct((b, s, d_model), BF16),
        scratch_shapes=[pltpu.VMEM((2, t, LANES), F32), pltpu.VMEM((2, t, LANES), F32)],
        compiler_params=pltpu.CompilerParams(
            dimension_semantics=("parallel", "parallel", "arbitrary"), vmem_limit_bytes=VMEM_LIMIT_BYTES),
        name="stickbreak_attn",
    )(qkv, qkv, qkv)


def _rnn_kernel(x_ref, g_ref, win_ref, cw_ref, cb_ref, wg_ref, ba_ref, bx_ref, lru_ref, y_ref,
                rec_tail, h_tail, a_scr, u_scr, *, width):
    ts = x_ref.shape[1]
    half = width // 2

    @pl.when(pl.program_id(1) == 0)
    def _():
        rec_tail[...] = jnp.zeros_like(rec_tail)
        h_tail[...] = jnp.zeros_like(h_tail)

    hn = _rms_norm(x_ref[0], g_ref[...]).astype(BF16)
    rec = _dot(hn, win_ref[:, width:])
    xr = _causal_conv(rec, rec_tail[...], cw_ref[...], cb_ref[...])
    rec_tail[...] = rec[ts - SUBLANES:, :]
    xrb = xr.astype(BF16)

    lru = lru_ref[...]
    log_sig = -(jnp.maximum(-lru, 0.0) + jnp.log1p(jnp.exp(-jnp.abs(lru))))
    rate = LRU_C * log_sig

    row = lax.broadcasted_iota(jnp.int32, (ts, half), 0) & (SUBLANES - 1)
    for hf in range(2):
        sl = slice(hf * half, (hf + 1) * half)
        pre = _dot(xrb[:, sl], wg_ref[hf])
        r = _sigmoid(pre[:, :half] + ba_ref[:, sl])
        i = _sigmoid(pre[:, half:] + bx_ref[:, sl])
        a = jnp.exp(r * rate[:, sl])
        u = jnp.sqrt(1.0 - a * a) * (i * xr[:, sl])
        for d in (1, 2, 4):
            keep = row >= d
            a_prev = pltpu.roll(a, d, axis=0)
            u_prev = pltpu.roll(u, d, axis=0)
            u = jnp.where(keep, a * u_prev + u, u)
            a = jnp.where(keep, a * a_prev, a)
        a_scr[:, sl] = a
        u_scr[:, sl] = u

    def group(gidx, h_prev):
        r0 = pl.multiple_of(gidx * SUBLANES, SUBLANES)
        h = u_scr[pl.ds(r0, SUBLANES), :] + a_scr[pl.ds(r0, SUBLANES), :] * h_prev
        u_scr[pl.ds(r0, SUBLANES), :] = h
        return jnp.broadcast_to(h[SUBLANES - 1:SUBLANES, :], h.shape)

    h_tail[...] = lax.fori_loop(0, ts // SUBLANES, group, h_tail[...])

    gate = _dot(hn, win_ref[:, :width])
    y_ref[0] = (_gelu(gate) * u_scr[...]).astype(BF16)


def _rnn_block(x, g, w_in, conv_w, conv_b, w_gates, b_a, b_x, lru):
    b, s, d = x.shape
    width = w_in.shape[1] // 2
    ts = min(RNN_TILE, s)
    const = lambda i, j: (0, 0)
    single = pl.Buffered(1)
    return pl.pallas_call(
        functools.partial(_rnn_kernel, width=width),
        grid=(b, s // ts),
        in_specs=[
            pl.BlockSpec((1, ts, d), lambda i, j: (i, j, 0)),
            pl.BlockSpec((1, d), const),
            pl.BlockSpec(w_in.shape, const, pipeline_mode=single),
            pl.BlockSpec(conv_w.shape, const),
            pl.BlockSpec((1, width), const),
            pl.BlockSpec(w_gates.shape, lambda i, j: (0, 0, 0), pipeline_mode=single),
            pl.BlockSpec((1, width), const),
            pl.BlockSpec((1, width), const),
            pl.BlockSpec((1, width), const),
        ],
        out_specs=pl.BlockSpec((1, ts, width), lambda i, j: (i, j, 0)),
        out_shape=jax.ShapeDtypeStruct((b, s, width), BF16),
        scratch_shapes=[
            pltpu.VMEM((SUBLANES, width), F32),
            pltpu.VMEM((SUBLANES, width), F32),
            pltpu.VMEM((ts, width), F32),
            pltpu.VMEM((ts, width), F32),
        ],
        compiler_params=pltpu.CompilerParams(
            dimension_semantics=("parallel", "arbitrary"), vmem_limit_bytes=VMEM_LIMIT_BYTES),
        name="rglru_block",
    )(x, g, w_in, conv_w, conv_b, w_gates, b_a, b_x, lru)


def _ffn_kernel(x_ref, m_ref, wm_ref, gf_ref, wup_ref, cw_ref, cb_ref, wdn_ref, gp_ref, wpg_ref, wpp_ref,
                p_ref, gfin_ref, o_ref, up_tail, acc_ref, *, ffn_dim, final_norm):
    @pl.when(pl.program_id(1) == 0)
    def _():
        up_tail[...] = jnp.zeros_like(up_tail)

    tm = x_ref.shape[1]
    x1 = x_ref[0] + _dot(m_ref[0], wm_ref[...])
    h = _rms_norm(x1, gf_ref[...]).astype(BF16)
    acc_ref[...] = x1
    for c in range(ffn_dim // FFN_CHUNK):
        parts = []
        for base in (0, ffn_dim):
            sl = slice(base + c * FFN_CHUNK, base + (c + 1) * FFN_CHUNK)
            up = _dot(h, wup_ref[:, sl])
            parts.append(_causal_conv(up, up_tail[:, sl], cw_ref[:, sl], cb_ref[:, sl]))
            up_tail[:, sl] = up[tm - SUBLANES:, :]
        act = (_gelu(parts[0]) * parts[1]).astype(BF16)
        acc_ref[...] += _dot(act, wdn_ref[c * FFN_CHUNK:(c + 1) * FFN_CHUNK, :])
    x2 = acc_ref[...]
    hp = _rms_norm(x2, gp_ref[...]).astype(BF16)
    gate = _sigmoid(_dot(hp, wpg_ref[...]))
    x3 = x2 + gate * _dot(p_ref[0, 0].astype(BF16), wpp_ref[...])
    if final_norm:
        x3 = _rms_norm(x3, gfin_ref[...])
    o_ref[0] = x3


def _ffn_block(x, mix, w_mix, g_ffn, w_up, conv_w, conv_b, w_down, g_ple, w_pgate, w_pproj, p, layer,
               g_final, final_norm):
    b, s, d = x.shape
    km = mix.shape[2]
    ffn_dim = w_down.shape[0]
    ple_dim = p.shape[3]
    assert ffn_dim % FFN_CHUNK == 0
    tm = min(ROW_TILE, s)
    const = lambda i, j: (0, 0)
    single = pl.Buffered(1)
    return pl.pallas_call(
        functools.partial(_ffn_kernel, ffn_dim=ffn_dim, final_norm=final_norm),
        grid=(b, s // tm),
        in_specs=[
            pl.BlockSpec((1, tm, d), lambda i, j: (i, j, 0)),
            pl.BlockSpec((1, tm, km), lambda i, j: (i, j, 0)),
            pl.BlockSpec((km, d), const, pipeline_mode=single),
            pl.BlockSpec((1, d), const),
            pl.BlockSpec((d, 2 * ffn_dim), const, pipeline_mode=single),
            pl.BlockSpec(conv_w.shape, const),
            pl.BlockSpec((1, 2 * ffn_dim), const),
            pl.BlockSpec((ffn_dim, d), const, pipeline_mode=single),
            pl.BlockSpec((1, d), const),
            pl.BlockSpec((d, d), const, pipeline_mode=single),
            pl.BlockSpec((ple_dim, d), const, pipeline_mode=single),
            pl.BlockSpec((1, 1, tm, ple_dim), lambda i, j: (layer, i, j, 0)),
            pl.BlockSpec((1, d), const),
        ],
        out_specs=pl.BlockSpec((1, tm, d), lambda i, j: (i, j, 0)),
        out_shape=jax.ShapeDtypeStruct((b, s, d), F32),
        scratch_shapes=[pltpu.VMEM((SUBLANES, 2 * ffn_dim), F32), pltpu.VMEM((tm, d), F32)],
        compiler_params=pltpu.CompilerParams(
            dimension_semantics=("parallel", "arbitrary"), vmem_limit_bytes=VMEM_LIMIT_BYTES),
        name="convffn_ple",
    )(x, mix, w_mix, g_ffn, w_up, conv_w, conv_b, w_down, g_ple, w_pgate, w_pproj, p, g_final)


def _block_diag_halves(w_a, w_x):
    heads, n, _ = w_a.shape
    hh = heads // 2
    eye = jnp.eye(hh, dtype=w_a.dtype)

    def dense(w):
        return jnp.einsum("hij,hg->higj", w, eye).reshape(hh * n, hh * n)

    halves = [jnp.concatenate([dense(w_a[hf * hh:(hf + 1) * hh]), dense(w_x[hf * hh:(hf + 1) * hh])], axis=1)
              for hf in range(2)]
    return jnp.stack(halves).astype(BF16)


def kernel(x, p, norm_mix, attn_w_qkv, attn_w_o, rnn_w_in, rnn_conv_w, rnn_conv_b, rnn_w_gate_a, rnn_b_gate_a, rnn_w_gate_x, rnn_b_gate_x, rnn_lru_param, rnn_w_out, norm_ffn, ffn_w_up, ffn_conv_w, ffn_conv_b, ffn_w_down, norm_ple, ple_w_gate, ple_w_proj, norm_final):
    depth = norm_mix.shape[0]
    d = x.shape[2]
    row = lambda v: v.reshape(1, -1)
    for i in range(depth):
        slot = i // 2
        if i % 2 == 0:
            qkv = _qkv_proj(x, row(norm_mix[i]), attn_w_qkv[slot].astype(BF16))
            mix = _attention(qkv, d)
            w_mix = attn_w_o[slot]
        else:
            mix = _rnn_block(
                x, row(norm_mix[i]), rnn_w_in[slot].astype(BF16), rnn_conv_w[slot], row(rnn_conv_b[slot]),
                _block_diag_halves(rnn_w_gate_a[slot], rnn_w_gate_x[slot]),
                row(rnn_b_gate_a[slot]), row(rnn_b_gate_x[slot]), row(rnn_lru_param[slot]))
            w_mix = rnn_w_out[slot]
        x = _ffn_block(
            x, mix, w_mix.astype(BF16), row(norm_ffn[i]), ffn_w_up[i].astype(BF16), ffn_conv_w[i],
            row(ffn_conv_b[i]), ffn_w_down[i].astype(BF16), row(norm_ple[i]), ple_w_gate[i].astype(BF16),
            ple_w_proj[i].astype(BF16), p, i, row(norm_final), final_norm=(i == depth - 1))
    return x
```

```python
import functools

import jax
import jax.numpy as jnp
from jax import lax
from jax.experimental import pallas as pl
from jax.experimental.pallas import tpu as pltpu

F32 = jnp.float32
BF16 = jnp.bfloat16

EPS = 1e-6
ATTN_HEADS = 16
RNN_HEADS = 16
LRU_C = 8.0

LANES = 128
SUBLANES = 8
MXU_DIM = 256
VMEM_LIMIT_BYTES = 56 * 1024 * 1024

ROW_TILE = 512
RNN_TILE = 256
ATTN_Q_TILE = 2 * MXU_DIM
ATTN_HEAD_GROUP = 8
FFN_CHUNK = MXU_DIM
LOG2E = 1.4426950408889634


def _rms_norm(x, g):
    ms = jnp.mean(x * x, axis=-1, keepdims=True)
    return (x * lax.rsqrt(ms + EPS)) * g


def _gelu_times(x, v):
    inner = x * (0.7978845608028654 + (0.7978845608028654 * 0.044715) * (x * x))
    return (x * v) * (0.5 + 0.5 * jnp.tanh(inner))


def _sigmoid(x):
    return 1.0 / (1.0 + jnp.exp(-x))


def _dot(a, b):
    return jnp.dot(a, b, preferred_element_type=F32)


def _causal_conv(cur, tail_ref, col0, win_ref, w, b):
    tm = cur.shape[0]
    k = w.shape[0]
    outs = []
    for s in range(cur.shape[1] // LANES):
        cs = slice(s * LANES, (s + 1) * LANES)
        gs = slice(col0 + s * LANES, col0 + (s + 1) * LANES)
        slab = cur[:, cs]
        win_ref[s, 0:SUBLANES, :] = tail_ref[:, gs]
        win_ref[s, SUBLANES:SUBLANES + tm, :] = slab
        tail_ref[:, gs] = slab[tm - SUBLANES:, :]
        out = slab * w[k - 1:k, cs] + b[:, cs]
        for d in range(1, k):
            out = out + win_ref[s, SUBLANES - d:SUBLANES - d + tm, :] * w[k - 1 - d:k - d, cs]
        outs.append(out)
    return jnp.concatenate(outs, axis=1)


def _qkv_kernel(x_ref, g_ref, w_ref, o_ref, *, d_model, q_scale):
    h = _rms_norm(x_ref[0], g_ref[...]).astype(BF16)
    for c in range(3):
        acc = _dot(h, w_ref[:, c * d_model:(c + 1) * d_model])
        if c == 0:
            acc = acc * q_scale
        o_ref[0, :, c * d_model:(c + 1) * d_model] = acc.astype(BF16)


def _qkv_proj(x, g, w):
    b, s, d = x.shape
    n = w.shape[1]
    tm = min(ROW_TILE, s)
    q_scale = (d // ATTN_HEADS) ** -0.5 * LOG2E
    return pl.pallas_call(
        functools.partial(_qkv_kernel, d_model=d, q_scale=q_scale),
        grid=(b, s // tm),
        in_specs=[
            pl.BlockSpec((1, tm, d), lambda i, j: (i, j, 0)),
            pl.BlockSpec((1, d), lambda i, j: (0, 0)),
            pl.BlockSpec((d, n), lambda i, j: (0, 0), pipeline_mode=pl.Buffered(1)),
        ],
        out_specs=pl.BlockSpec((1, tm, n), lambda i, j: (i, j, 0)),
        out_shape=jax.ShapeDtypeStruct((b, s, n), BF16),
        compiler_params=pltpu.CompilerParams(
            dimension_semantics=("parallel", "parallel"), vmem_limit_bytes=VMEM_LIMIT_BYTES),
        name="qkv_proj",
    )(x, g, w)


def _attn_kernel(q_ref, k_ref, v_ref, o_ref, acc_ref, car_ref, *, head_dim):
    qi = pl.program_id(2)
    tq = q_ref.shape[1]
    tk = tq // 2
    pairs = q_ref.shape[2] // LANES
    low_q = lax.broadcasted_iota(jnp.int32, (tq, LANES), 1) < head_dim
    low_k = lax.broadcasted_iota(jnp.int32, (tk, LANES), 1) < head_dim
    q_heads = []
    for p in range(pairs):
        qp = q_ref[0, :, p * LANES:(p + 1) * LANES]
        zero = jnp.zeros_like(qp)
        q_heads += [jnp.where(low_q, qp, zero), jnp.where(low_q, zero, qp)]
    row = lax.broadcasted_iota(jnp.int32, (tk, tk), 0)
    col = lax.broadcasted_iota(jnp.int32, (tk, tk), 1)
    suffix = jnp.where(row >= col, 1.0, 0.0).astype(BF16)
    causal = col < row
    acc_ref[...] = jnp.zeros_like(acc_ref)
    car_ref[...] = jnp.zeros_like(car_ref)

    def block(r0, nrow, j, masked):
        start = pl.multiple_of(j * tk, tk)
        rows = slice(r0, r0 + nrow)
        for p in range(pairs):
            k = k_ref[0, pl.ds(start, tk), p * LANES:(p + 1) * LANES]
            v = v_ref[0, pl.ds(start, tk), p * LANES:(p + 1) * LANES]
            zero = jnp.zeros_like(v)
            v2 = jnp.concatenate([jnp.where(low_k, v, zero), jnp.where(low_k, zero, v)], axis=0)
            ws = []
            for h in (2 * p, 2 * p + 1):
                z2 = lax.dot_general(q_heads[h][rows], k, (((1,), (1,)), ((), ())), preferred_element_type=F32)
                neg_abs = lax.bitcast_convert_type(
                    lax.bitcast_convert_type(z2, jnp.uint32) | jnp.uint32(0x80000000), F32)
                pen = jnp.maximum(z2, 0.0) + jnp.log(1.0 + jnp.exp2(neg_abs)) * LOG2E
                if masked:
                    pen = jnp.where(causal, pen, 0.0)
                csum = _dot(pen.astype(BF16), suffix)
                car = car_ref[h, rows]
                w = jnp.exp2(z2 - csum - jnp.concatenate([car] * (tk // LANES), axis=1))
                if masked:
                    w = jnp.where(causal, w, 0.0)
                ws.append(w.astype(BF16))
                car_ref[h, rows] = car + jnp.broadcast_to(csum[:, 0:1], (nrow, LANES))
            acc_ref[p, rows] += _dot(jnp.concatenate(ws, axis=1), v2)

    block(tk, tk, 2 * qi + 1, True)
    block(tk, tk, 2 * qi, False)
    block(0, tk, 2 * qi, True)

    def body(it, carry):
        block(0, tq, 2 * qi - 1 - it, False)
        return carry

    lax.fori_loop(0, 2 * qi, body, 0)
    for p in range(pairs):
        o_ref[0, :, p * LANES:(p + 1) * LANES] = acc_ref[p].astype(BF16)


def _attention(qkv, d_model):
    b, s, _ = qkv.shape
    head_dim = d_model // ATTN_HEADS
    assert 2 * head_dim == LANES
    tq = min(ATTN_Q_TILE, s)
    width = ATTN_HEAD_GROUP * head_dim
    groups = d_model // width
    pairs = width // LANES
    return pl.pallas_call(
        functools.partial(_attn_kernel, head_dim=head_dim),
        grid=(b, groups, s // tq),
        in_specs=[
            pl.BlockSpec((1, tq, width), lambda i, g, j: (i, j, g)),
            pl.BlockSpec((1, s, width), lambda i, g, j: (i, 0, groups + g)),
            pl.BlockSpec((1, s, width), lambda i, g, j: (i, 0, 2 * groups + g)),
        ],
        out_specs=pl.BlockSpec((1, tq, width), lambda i, g, j: (i, j, g)),
        out_shape=jax.ShapeDtypeStruct((b, s, d_model), BF16),
        scratch_shapes=[pltpu.VMEM((pairs, tq, LANES), F32), pltpu.VMEM((2 * pairs, tq, LANES), F32)],
        compiler_params=pltpu.CompilerParams(
            dimension_semantics=("parallel", "parallel", "arbitrary"), vmem_limit_bytes=VMEM_LIMIT_BYTES),
        name="stickbreak_attn",
    )(qkv, qkv, qkv)


def _rnn_kernel(x_ref, g_ref, win_ref, cw_ref, cb_ref, wg_ref, ba_ref, bx_ref, lru_ref, y_ref,
                rec_tail, h_tail, a_scr, u_scr, conv_scr, *, width):
    ts = x_ref.shape[1]
    half = width // 2

    @pl.when(pl.program_id(1) == 0)
    def _():
        rec_tail[...] = jnp.zeros_like(rec_tail)
        h_tail[...] = jnp.zeros_like(h_tail)

    hn = _rms_norm(x_ref[0], g_ref[...]).astype(BF16)
    rec = _dot(hn, win_ref[:, width:])
    xr = _causal_conv(rec, rec_tail, 0, conv_scr, cw_ref[...], cb_ref[...])
    xrb = xr.astype(BF16)

    lru = lru_ref[...]
    log_sig = -(jnp.maximum(-lru, 0.0) + jnp.log1p(jnp.exp(-jnp.abs(lru))))
    rate = LRU_C * log_sig

    row = lax.broadcasted_iota(jnp.int32, (ts, half), 0) & (SUBLANES - 1)
    for hf in range(2):
        sl = slice(hf * half, (hf + 1) * half)
        pre = _dot(xrb[:, sl], wg_ref[hf])
        r = _sigmoid(pre[:, :half] + ba_ref[:, sl])
        i = _sigmoid(pre[:, half:] + bx_ref[:, sl])
        a = jnp.exp(r * rate[:, sl])
        u = jnp.sqrt(1.0 - a * a) * (i * xr[:, sl])
        for d in (1, 2, 4):
            keep = row >= d
            a_prev = pltpu.roll(a, d, axis=0)
            u_prev = pltpu.roll(u, d, axis=0)
            u = jnp.where(keep, a * u_prev + u, u)
            a = jnp.where(keep, a * a_prev, a)
        a_scr[:, sl] = a
        u_scr[:, sl] = u

    def group(gidx, h_prev):
        r0 = pl.multiple_of(gidx * SUBLANES, SUBLANES)
        h = u_scr[pl.ds(r0, SUBLANES), :] + a_scr[pl.ds(r0, SUBLANES), :] * h_prev
        u_scr[pl.ds(r0, SUBLANES), :] = h
        return jnp.broadcast_to(h[SUBLANES - 1:SUBLANES, :], h.shape)

    h_tail[...] = lax.fori_loop(0, ts // SUBLANES, group, h_tail[...])

    gate = _dot(hn, win_ref[:, :width])
    y_ref[0] = _gelu_times(gate, u_scr[...]).astype(BF16)


def _rnn_block(x, g, w_in, conv_w, conv_b, w_gates, b_a, b_x, lru):
    b, s, d = x.shape
    width = w_in.shape[1] // 2
    ts = min(RNN_TILE, s)
    const = lambda i, j: (0, 0)
    single = pl.Buffered(1)
    return pl.pallas_call(
        functools.partial(_rnn_kernel, width=width),
        grid=(b, s // ts),
        in_specs=[
            pl.BlockSpec((1, ts, d), lambda i, j: (i, j, 0)),
            pl.BlockSpec((1, d), const),
            pl.BlockSpec(w_in.shape, const, pipeline_mode=single),
            pl.BlockSpec(conv_w.shape, const),
            pl.BlockSpec((1, width), const),
            pl.BlockSpec(w_gates.shape, lambda i, j: (0, 0, 0), pipeline_mode=single),
            pl.BlockSpec((1, width), const),
            pl.BlockSpec((1, width), const),
            pl.BlockSpec((1, width), const),
        ],
        out_specs=pl.BlockSpec((1, ts, width), lambda i, j: (i, j, 0)),
        out_shape=jax.ShapeDtypeStruct((b, s, width), BF16),
        scratch_shapes=[
            pltpu.VMEM((SUBLANES, width), F32),
            pltpu.VMEM((SUBLANES, width), F32),
            pltpu.VMEM((ts, width), F32),
            pltpu.VMEM((ts, width), F32),
            pltpu.VMEM((width // LANES, SUBLANES + ts, LANES), F32),
        ],
        compiler_params=pltpu.CompilerParams(
            dimension_semantics=("parallel", "arbitrary"), vmem_limit_bytes=VMEM_LIMIT_BYTES),
        name="rglru_block",
    )(x, g, w_in, conv_w, conv_b, w_gates, b_a, b_x, lru)


def _ffn_kernel(x_ref, m_ref, wm_ref, gf_ref, wup_ref, cw_ref, cb_ref, wdn_ref, gp_ref, wpg_ref, wpp_ref,
                p_ref, gfin_ref, o_ref, up_tail, win_ref, act_ref, *, ffn_dim, final_norm):
    @pl.when(pl.program_id(1) == 0)
    def _():
        up_tail[...] = jnp.zeros_like(up_tail)

    x1 = x_ref[0] + _dot(m_ref[0], wm_ref[...])
    h = _rms_norm(x1, gf_ref[...]).astype(BF16)
    for c in range(ffn_dim // FFN_CHUNK):
        parts = []
        for part, base in enumerate((0, ffn_dim)):
            col0 = base + c * FFN_CHUNK
            sl = slice(col0, col0 + FFN_CHUNK)
            up = _dot(h, wup_ref[:, sl])
            parts.append(_causal_conv(up, up_tail, col0, win_ref.at[c % 2, part], cw_ref[:, sl], cb_ref[:, sl]))
        act_ref[:, c * FFN_CHUNK:(c + 1) * FFN_CHUNK] = _gelu_times(parts[0], parts[1]).astype(BF16)
    x2 = x1 + _dot(act_ref[...], wdn_ref[...])
    hp = _rms_norm(x2, gp_ref[...]).astype(BF16)
    gate = _sigmoid(_dot(hp, wpg_ref[...]))
    x3 = x2 + gate * _dot(p_ref[0, 0].astype(BF16), wpp_ref[...])
    if final_norm:
        x3 = _rms_norm(x3, gfin_ref[...])
    o_ref[0] = x3


def _ffn_block(x, mix, w_mix, g_ffn, w_up, conv_w, conv_b, w_down, g_ple, w_pgate, w_pproj, p, layer,
               g_final, final_norm):
    b, s, d = x.shape
    km = mix.shape[2]
    ffn_dim = w_down.shape[0]
    ple_dim = p.shape[3]
    assert ffn_dim % FFN_CHUNK == 0
    tm = min(ROW_TILE, s)
    const = lambda i, j: (0, 0)
    single = pl.Buffered(1)
    return pl.pallas_call(
        functools.partial(_ffn_kernel, ffn_dim=ffn_dim, final_norm=final_norm),
        grid=(b, s // tm),
        in_specs=[
            pl.BlockSpec((1, tm, d), lambda i, j: (i, j, 0)),
            pl.BlockSpec((1, tm, km), lambda i, j: (i, j, 0)),
            pl.BlockSpec((km, d), const, pipeline_mode=single),
            pl.BlockSpec((1, d), const),
            pl.BlockSpec((d, 2 * ffn_dim), const, pipeline_mode=single),
            pl.BlockSpec(conv_w.shape, const),
            pl.BlockSpec((1, 2 * ffn_dim), const),
            pl.BlockSpec((ffn_dim, d), const, pipeline_mode=single),
            pl.BlockSpec((1, d), const),
            pl.BlockSpec((d, d), const, pipeline_mode=single),
            pl.BlockSpec((ple_dim, d), const, pipeline_mode=single),
            pl.BlockSpec((1, 1, tm, ple_dim), lambda i, j: (layer, i, j, 0)),
            pl.BlockSpec((1, d), const),
        ],
        out_specs=pl.BlockSpec((1, tm, d), lambda i, j: (i, j, 0)),
        out_shape=jax.ShapeDtypeStruct((b, s, d), F32),
        scratch_shapes=[
            pltpu.VMEM((SUBLANES, 2 * ffn_dim), F32),
            pltpu.VMEM((2, 2, FFN_CHUNK // LANES, SUBLANES + tm, LANES), F32),
            pltpu.VMEM((tm, ffn_dim), BF16),
        ],
        compiler_params=pltpu.CompilerParams(
            dimension_semantics=("parallel", "arbitrary"), vmem_limit_bytes=VMEM_LIMIT_BYTES),
        name="convffn_ple",
    )(x, mix, w_mix, g_ffn, w_up, conv_w, conv_b, w_down, g_ple, w_pgate, w_pproj, p, g_final)


def _block_diag_halves(w_a, w_x):
    heads, n, _ = w_a.shape
    hh = heads // 2
    eye = jnp.eye(hh, dtype=w_a.dtype)

    def dense(w):
        return jnp.einsum("hij,hg->higj", w, eye).reshape(hh * n, hh * n)

    halves = [jnp.concatenate([dense(w_a[hf * hh:(hf + 1) * hh]), dense(w_x[hf * hh:(hf + 1) * hh])], axis=1)
              for hf in range(2)]
    return jnp.stack(halves).astype(BF16)


def kernel(x, p, norm_mix, attn_w_qkv, attn_w_o, rnn_w_in, rnn_conv_w, rnn_conv_b, rnn_w_gate_a, rnn_b_gate_a, rnn_w_gate_x, rnn_b_gate_x, rnn_lru_param, rnn_w_out, norm_ffn, ffn_w_up, ffn_conv_w, ffn_conv_b, ffn_w_down, norm_ple, ple_w_gate, ple_w_proj, norm_final):
    depth = norm_mix.shape[0]
    d = x.shape[2]
    row = lambda v: v.reshape(1, -1)
    for i in range(depth):
        slot = i // 2
        if i % 2 == 0:
            qkv = _qkv_proj(x, row(norm_mix[i]), attn_w_qkv[slot].astype(BF16))
            mix = _attention(qkv, d)
            w_mix = attn_w_o[slot]
        else:
            mix = _rnn_block(
                x, row(norm_mix[i]), rnn_w_in[slot].astype(BF16), rnn_conv_w[slot], row(rnn_conv_b[slot]),
                _block_diag_halves(rnn_w_gate_a[slot], rnn_w_gate_x[slot]),
                row(rnn_b_gate_a[slot]), row(rnn_b_gate_x[slot]), row(rnn_lru_param[slot]))
            w_mix = rnn_w_out[slot]
        x = _ffn_block(
            x, mix, w_mix.astype(BF16), row(norm_ffn[i]), ffn_w_up[i].astype(BF16), ffn_conv_w[i],
            row(ffn_conv_b[i]), ffn_w_down[i].astype(BF16), row(norm_ple[i]), ple_w_gate[i].astype(BF16),
            ple_w_proj[i].astype(BF16), p, i, row(norm_final), final_norm=(i == depth - 1))
    return x
```

```python
import functools

import jax
import jax.numpy as jnp
from jax import lax
from jax.experimental import pallas as pl
from jax.experimental.pallas import tpu as pltpu

F32 = jnp.float32
BF16 = jnp.bfloat16

EPS = 1e-6
ATTN_HEADS = 16
RNN_HEADS = 16
LRU_C = 8.0

LANES = 128
SUBLANES = 8
MXU_DIM = 256
VMEM_LIMIT_BYTES = 56 * 1024 * 1024

ROW_TILE = 512
RNN_TILE = 256
ATTN_Q_TILE = 2 * MXU_DIM
ATTN_HEAD_GROUP = 8
FFN_CHUNK = MXU_DIM
LOG2E = 1.4426950408889634


def _rms_norm(x, g):
    ms = jnp.mean(x * x, axis=-1, keepdims=True)
    return (x * lax.rsqrt(ms + EPS)) * g


def _gelu_times(x, v):
    inner = x * (0.7978845608028654 + (0.7978845608028654 * 0.044715) * (x * x))
    return (x * v) * (0.5 + 0.5 * jnp.tanh(inner))


def _sigmoid(x):
    return 1.0 / (1.0 + jnp.exp(-x))


def _dot(a, b):
    return jnp.dot(a, b, preferred_element_type=F32)


def _causal_conv(cur, tail_ref, col0, win_ref, w, b):
    tm = cur.shape[0]
    k = w.shape[0]
    outs = []
    for s in range(cur.shape[1] // LANES):
        cs = slice(s * LANES, (s + 1) * LANES)
        gs = slice(col0 + s * LANES, col0 + (s + 1) * LANES)
        slab = cur[:, cs]
        win_ref[s, 0:SUBLANES, :] = tail_ref[:, gs]
        win_ref[s, SUBLANES:SUBLANES + tm, :] = slab
        tail_ref[:, gs] = slab[tm - SUBLANES:, :]
        out = slab * w[k - 1:k, cs] + b[:, cs]
        for d in range(1, k):
            out = out + win_ref[s, SUBLANES - d:SUBLANES - d + tm, :] * w[k - 1 - d:k - d, cs]
        outs.append(out)
    return jnp.concatenate(outs, axis=1)


def _qkv_kernel(x_ref, g_ref, w_ref, o_ref, *, d_model, q_scale):
    h = _rms_norm(x_ref[0], g_ref[...]).astype(BF16)
    for c in range(3):
        acc = _dot(h, w_ref[:, c * d_model:(c + 1) * d_model])
        if c == 0:
            acc = acc * q_scale
        o_ref[0, :, c * d_model:(c + 1) * d_model] = acc.astype(BF16)


def _qkv_proj(x, g, w):
    b, s, d = x.shape
    n = w.shape[1]
    tm = min(ROW_TILE, s)
    q_scale = (d // ATTN_HEADS) ** -0.5 * LOG2E
    return pl.pallas_call(
        functools.partial(_qkv_kernel, d_model=d, q_scale=q_scale),
        grid=(b, s // tm),
        in_specs=[
            pl.BlockSpec((1, tm, d), lambda i, j: (i, j, 0)),
            pl.BlockSpec((1, d), lambda i, j: (0, 0)),
            pl.BlockSpec((d, n), lambda i, j: (0, 0), pipeline_mode=pl.Buffered(1)),
        ],
        out_specs=pl.BlockSpec((1, tm, n), lambda i, j: (i, j, 0)),
        out_shape=jax.ShapeDtypeStruct((b, s, n), BF16),
        compiler_params=pltpu.CompilerParams(
            dimension_semantics=("parallel", "parallel"), vmem_limit_bytes=VMEM_LIMIT_BYTES),
        name="qkv_proj",
    )(x, g, w)


def _attn_kernel(q_ref, k_ref, v_ref, o_ref, acc_ref, car_ref, *, head_dim):
    qi = pl.program_id(2)
    tq = q_ref.shape[1]
    tk = tq // 2
    pairs = q_ref.shape[2] // LANES
    low_q = lax.broadcasted_iota(jnp.int32, (tq, LANES), 1) < head_dim
    low_k = lax.broadcasted_iota(jnp.int32, (tk, LANES), 1) < head_dim
    q_heads = []
    for p in range(pairs):
        qp = q_ref[0, :, p * LANES:(p + 1) * LANES]
        zero = jnp.zeros_like(qp)
        q_heads += [jnp.where(low_q, qp, zero), jnp.where(low_q, zero, qp)]
    row = lax.broadcasted_iota(jnp.int32, (tk, tk), 0)
    col = lax.broadcasted_iota(jnp.int32, (tk, tk), 1)
    suffix = jnp.where(row >= col, 1.0, 0.0).astype(BF16)
    causal = {tk: col < row,
              tq: lax.broadcasted_iota(jnp.int32, (tq, tk), 1) < lax.broadcasted_iota(jnp.int32, (tq, tk), 0)}
    acc_ref[...] = jnp.zeros_like(acc_ref)
    car_ref[...] = jnp.zeros_like(car_ref)

    def block(r0, nrow, j, masked):
        start = pl.multiple_of(j * tk, tk)
        rows = slice(r0, r0 + nrow)
        for p in range(pairs):
            k = k_ref[0, pl.ds(start, tk), p * LANES:(p + 1) * LANES]
            v = v_ref[0, pl.ds(start, tk), p * LANES:(p + 1) * LANES]
            zero = jnp.zeros_like(v)
            v2 = jnp.concatenate([jnp.where(low_k, v, zero), jnp.where(low_k, zero, v)], axis=0)
            ws = []
            for h in (2 * p, 2 * p + 1):
                z2 = lax.dot_general(q_heads[h][rows], k, (((1,), (1,)), ((), ())), preferred_element_type=F32)
                zb = z2.astype(BF16)
                pen = jnp.maximum(zb, 0.0) + jnp.log(1.0 + jnp.exp2(-jnp.abs(zb))) * LOG2E
                if masked:
                    pen = jnp.where(causal[nrow], pen, jnp.zeros_like(pen))
                csum = _dot(pen, suffix)
                car = car_ref[h, rows]
                w = jnp.exp2(z2 - csum - jnp.concatenate([car] * (tk // LANES), axis=1))
                if masked:
                    w = jnp.where(causal[nrow], w, 0.0)
                ws.append(w.astype(BF16))
                car_ref[h, rows] = car + jnp.broadcast_to(csum[:, 0:1], (nrow, LANES))
            acc_ref[p, rows] += _dot(jnp.concatenate(ws, axis=1), v2)

    block(tk, tk, 2 * qi + 1, True)
    block(0, tq, 2 * qi, True)

    def body(it, carry):
        block(0, tq, 2 * qi - 1 - it, False)
        return carry

    lax.fori_loop(0, 2 * qi, body, 0)
    for p in range(pairs):
        o_ref[0, :, p * LANES:(p + 1) * LANES] = acc_ref[p].astype(BF16)


def _attention(qkv, d_model):
    b, s, _ = qkv.shape
    head_dim = d_model // ATTN_HEADS
    assert 2 * head_dim == LANES
    tq = min(ATTN_Q_TILE, s)
    width = ATTN_HEAD_GROUP * head_dim
    groups = d_model // width
    pairs = width // LANES
    return pl.pallas_call(
        functools.partial(_attn_kernel, head_dim=head_dim),
        grid=(b, groups, s // tq),
        in_specs=[
            pl.BlockSpec((1, tq, width), lambda i, g, j: (i, j, g)),
            pl.BlockSpec((1, s, width), lambda i, g, j: (i, 0, groups + g)),
            pl.BlockSpec((1, s, width), lambda i, g, j: (i, 0, 2 * groups + g)),
        ],
        out_specs=pl.BlockSpec((1, tq, width), lambda i, g, j: (i, j, g)),
        out_shape=jax.ShapeDtypeStruct((b, s, d_model), BF16),
        scratch_shapes=[pltpu.VMEM((pairs, tq, LANES), F32), pltpu.VMEM((2 * pairs, tq, LANES), F32)],
        compiler_params=pltpu.CompilerParams(
            dimension_semantics=("parallel", "parallel", "arbitrary"), vmem_limit_bytes=VMEM_LIMIT_BYTES),
        name="stickbreak_attn",
    )(qkv, qkv, qkv)


def _rnn_kernel(x_ref, g_ref, win_ref, cw_ref, cb_ref, wg_ref, ba_ref, bx_ref, lru_ref, y_ref,
                rec_tail, h_tail, a_scr, u_scr, conv_scr, *, width):
    ts = x_ref.shape[1]
    half = width // 2

    @pl.when(pl.program_id(1) == 0)
    def _():
        rec_tail[...] = jnp.zeros_like(rec_tail)
        h_tail[...] = jnp.zeros_like(h_tail)

    hn = _rms_norm(x_ref[0], g_ref[...]).astype(BF16)
    rec = _dot(hn, win_ref[:, width:])
    xr = _causal_conv(rec, rec_tail, 0, conv_scr, cw_ref[...], cb_ref[...])
    xrb = xr.astype(BF16)

    lru = lru_ref[...]
    log_sig = -(jnp.maximum(-lru, 0.0) + jnp.log1p(jnp.exp(-jnp.abs(lru))))
    rate = LRU_C * log_sig

    clen = ts // SUBLANES
    pitch = a_scr.shape[1] // SUBLANES
    slabs_per_half = half // LANES
    for hf in range(2):
        sl = slice(hf * half, (hf + 1) * half)
        pre = _dot(xrb[:, sl], wg_ref[hf])
        r = _sigmoid(pre[:, :half] + ba_ref[:, sl])
        i = _sigmoid(pre[:, half:] + bx_ref[:, sl])
        a = jnp.exp(r * rate[:, sl])
        u = jnp.sqrt(1.0 - a * a) * (i * xr[:, sl])
        for sh in range(slabs_per_half):
            s = hf * slabs_per_half + sh
            for c in range(SUBLANES):
                a_scr[s, c * pitch:c * pitch + clen, :] = a[c * clen:(c + 1) * clen, sh * LANES:(sh + 1) * LANES]
                u_scr[s, c * pitch:c * pitch + clen, :] = u[c * clen:(c + 1) * clen, sh * LANES:(sh + 1) * LANES]

    gate = _dot(hn, win_ref[:, :width])
    for s in range(width // LANES):
        cols = slice(s * LANES, (s + 1) * LANES)
        h = jnp.zeros((SUBLANES, LANES), F32)
        p = jnp.ones((SUBLANES, LANES), F32)
        for step in range(clen):
            idx = pl.ds(step, SUBLANES, stride=pitch)
            av = a_scr[s, idx, :]
            h = av * h + u_scr[s, idx, :]
            p = av * p
            u_scr[s, idx, :] = h
            a_scr[s, idx, :] = p
        enter = h_tail[:, cols]
        for c in range(SUBLANES):
            rows = slice(c * pitch, c * pitch + clen)
            h_full = u_scr[s, rows, :] + a_scr[s, rows, :] * enter
            out_rows = slice(c * clen, (c + 1) * clen)
            y_ref[0, out_rows, cols] = _gelu_times(gate[out_rows, cols], h_full).astype(BF16)
            enter = p[c:c + 1, :] * enter + h[c:c + 1, :]
        h_tail[:, cols] = enter


def _rnn_block(x, g, w_in, conv_w, conv_b, w_gates, b_a, b_x, lru):
    b, s, d = x.shape
    width = w_in.shape[1] // 2
    ts = min(RNN_TILE, s)
    scan_rows = SUBLANES * (ts // SUBLANES + SUBLANES // 2)
    const = lambda i, j: (0, 0)
    single = pl.Buffered(1)
    return pl.pallas_call(
        functools.partial(_rnn_kernel, width=width),
        grid=(b, s // ts),
        in_specs=[
            pl.BlockSpec((1, ts, d), lambda i, j: (i, j, 0)),
            pl.BlockSpec((1, d), const),
            pl.BlockSpec(w_in.shape, const, pipeline_mode=single),
            pl.BlockSpec(conv_w.shape, const),
            pl.BlockSpec((1, width), const),
            pl.BlockSpec(w_gates.shape, lambda i, j: (0, 0, 0), pipeline_mode=single),
            pl.BlockSpec((1, width), const),
            pl.BlockSpec((1, width), const),
            pl.BlockSpec((1, width), const),
        ],
        out_specs=pl.BlockSpec((1, ts, width), lambda i, j: (i, j, 0)),
        out_shape=jax.ShapeDtypeStruct((b, s, width), BF16),
        scratch_shapes=[
            pltpu.VMEM((SUBLANES, width), F32),
            pltpu.VMEM((1, width), F32),
            pltpu.VMEM((width // LANES, scan_rows, LANES), F32),
            pltpu.VMEM((width // LANES, scan_rows, LANES), F32),
            pltpu.VMEM((width // LANES, SUBLANES + ts, LANES), F32),
        ],
        compiler_params=pltpu.CompilerParams(
            dimension_semantics=("parallel", "arbitrary"), vmem_limit_bytes=VMEM_LIMIT_BYTES),
        name="rglru_block",
    )(x, g, w_in, conv_w, conv_b, w_gates, b_a, b_x, lru)


def _ffn_kernel(x_ref, m_ref, wm_ref, gf_ref, wup_ref, cw_ref, cb_ref, wdn_ref, gp_ref, wpg_ref, wpp_ref,
                p_ref, gfin_ref, o_ref, up_tail, win_ref, act_ref, *, ffn_dim, final_norm):
    @pl.when(pl.program_id(1) == 0)
    def _():
        up_tail[...] = jnp.zeros_like(up_tail)

    x1 = x_ref[0] + _dot(m_ref[0], wm_ref[...])
    h = _rms_norm(x1, gf_ref[...]).astype(BF16)
    for c in range(ffn_dim // FFN_CHUNK):
        parts = []
        for part, base in enumerate((0, ffn_dim)):
            col0 = base + c * FFN_CHUNK
            sl = slice(col0, col0 + FFN_CHUNK)
            up = _dot(h, wup_ref[:, sl])
            parts.append(_causal_conv(up, up_tail, col0, win_ref.at[c % 2, part], cw_ref[:, sl], cb_ref[:, sl]))
        act_ref[:, c * FFN_CHUNK:(c + 1) * FFN_CHUNK] = _gelu_times(parts[0], parts[1]).astype(BF16)
    x2 = x1 + _dot(act_ref[...], wdn_ref[...])
    hp = _rms_norm(x2, gp_ref[...]).astype(BF16)
    gate = _sigmoid(_dot(hp, wpg_ref[...]))
    x3 = x2 + gate * _dot(p_ref[0, 0].astype(BF16), wpp_ref[...])
    if final_norm:
        x3 = _rms_norm(x3, gfin_ref[...])
    o_ref[0] = x3


def _ffn_block(x, mix, w_mix, g_ffn, w_up, conv_w, conv_b, w_down, g_ple, w_pgate, w_pproj, p, layer,
               g_final, final_norm):
    b, s, d = x.shape
    km = mix.shape[2]
    ffn_dim = w_down.shape[0]
    ple_dim = p.shape[3]
    assert ffn_dim % FFN_CHUNK == 0
    tm = min(ROW_TILE, s)
    const = lambda i, j: (0, 0)
    single = pl.Buffered(1)
    return pl.pallas_call(
        functools.partial(_ffn_kernel, ffn_dim=ffn_dim, final_norm=final_norm),
        grid=(b, s // tm),
        in_specs=[
            pl.BlockSpec((1, tm, d), lambda i, j: (i, j, 0)),
            pl.BlockSpec((1, tm, km), lambda i, j: (i, j, 0)),
            pl.BlockSpec((km, d), const, pipeline_mode=single),
            pl.BlockSpec((1, d), const),
            pl.BlockSpec((d, 2 * ffn_dim), const, pipeline_mode=single),
            pl.BlockSpec(conv_w.shape, const),
            pl.BlockSpec((1, 2 * ffn_dim), const),
            pl.BlockSpec((ffn_dim, d), const, pipeline_mode=single),
            pl.BlockSpec((1, d), const),
            pl.BlockSpec((d, d), const, pipeline_mode=single),
            pl.BlockSpec((ple_dim, d), const, pipeline_mode=single),
            pl.BlockSpec((1, 1, tm, ple_dim), lambda i, j: (layer, i, j, 0)),
            pl.BlockSpec((1, d), const),
        ],
        out_specs=pl.BlockSpec((1, tm, d), lambda i, j: (i, j, 0)),
        out_shape=jax.ShapeDtypeStruct((b, s, d), F32),
        scratch_shapes=[
            pltpu.VMEM((SUBLANES, 2 * ffn_dim), F32),
            pltpu.VMEM((2, 2, FFN_CHUNK // LANES, SUBLANES + tm, LANES), F32),
            pltpu.VMEM((tm, ffn_dim), BF16),
        ],
        compiler_params=pltpu.CompilerParams(
            dimension_semantics=("parallel", "arbitrary"), vmem_limit_bytes=VMEM_LIMIT_BYTES),
        name="convffn_ple",
    )(x, mix, w_mix, g_ffn, w_up, conv_w, conv_b, w_down, g_ple, w_pgate, w_pproj, p, g_final)


def _block_diag_halves(w_a, w_x):
    heads, n, _ = w_a.shape
    hh = heads // 2
    eye = jnp.eye(hh, dtype=w_a.dtype)

    def dense(w):
        return jnp.einsum("hij,hg->higj", w, eye).reshape(hh * n, hh * n)

    halves = [jnp.concatenate([dense(w_a[hf * hh:(hf + 1) * hh]), dense(w_x[hf * hh:(hf + 1) * hh])], axis=1)
              for hf in range(2)]
    return jnp.stack(halves).astype(BF16)


def kernel(x, p, norm_mix, attn_w_qkv, attn_w_o, rnn_w_in, rnn_conv_w, rnn_conv_b, rnn_w_gate_a, rnn_b_gate_a, rnn_w_gate_x, rnn_b_gate_x, rnn_lru_param, rnn_w_out, norm_ffn, ffn_w_up, ffn_conv_w, ffn_conv_b, ffn_w_down, norm_ple, ple_w_gate, ple_w_proj, norm_final):
    depth = norm_mix.shape[0]
    d = x.shape[2]
    row = lambda v: v.reshape(1, -1)
    for i in range(depth):
        slot = i // 2
        if i % 2 == 0:
            qkv = _qkv_proj(x, row(norm_mix[i]), attn_w_qkv[slot].astype(BF16))
            mix = _attention(qkv, d)
            w_mix = attn_w_o[slot]
        else:
            mix = _rnn_block(
                x, row(norm_mix[i]), rnn_w_in[slot].astype(BF16), rnn_conv_w[slot], row(rnn_conv_b[slot]),
                _block_diag_halves(rnn_w_gate_a[slot], rnn_w_gate_x[slot]),
                row(rnn_b_gate_a[slot]), row(rnn_b_gate_x[slot]), row(rnn_lru_param[slot]))
            w_mix = rnn_w_out[slot]
        x = _ffn_block(
            x, mix, w_mix.astype(BF16), row(norm_ffn[i]), ffn_w_up[i].astype(BF16), ffn_conv_w[i],
            row(ffn_conv_b[i]), ffn_w_down[i].astype(BF16), row(norm_ple[i]), ple_w_gate[i].astype(BF16),
            ple_w_proj[i].astype(BF16), p, i, row(norm_final), final_norm=(i == depth - 1))
    return x
```

```python
import functools

import jax
import jax.numpy as jnp
from jax import lax
from jax.experimental import pallas as pl
from jax.experimental.pallas import tpu as pltpu

F32 = jnp.float32
BF16 = jnp.bfloat16

EPS = 1e-6
ATTN_HEADS = 16
RNN_HEADS = 16
LRU_C = 8.0

LANES = 128
SUBLANES = 8
MXU_DIM = 256
VMEM_LIMIT_BYTES = 56 * 1024 * 1024

ROW_TILE = 512
RNN_TILE = 512
RNN_SUBTILE = 256
QKV_TILE = 1024
ATTN_Q_TILE = 2 * MXU_DIM
ATTN_HEAD_GROUP = 8
FFN_CHUNK = MXU_DIM
LOG2E = 1.4426950408889634
MIN_NORMAL = 1.1754943508222875e-38


def _rms_norm(x, g):
    ms = jnp.mean(x * x, axis=-1, keepdims=True)
    return (x * lax.rsqrt(ms + EPS)) * g


def _gelu_times(x, v):
    inner = x * (0.7978845608028654 + (0.7978845608028654 * 0.044715) * (x * x))
    return (x * v) * (0.5 + 0.5 * jnp.tanh(inner))


def _sigmoid(x):
    return 1.0 / (1.0 + jnp.exp2(x * (-LOG2E)))


def _sqrt_unit_interval(y):
    return y * lax.rsqrt(jnp.maximum(y, MIN_NORMAL))


def _dot(a, b):
    return jnp.dot(a, b, preferred_element_type=F32)


def _causal_conv(cur, tail_ref, col0, win_ref, w, b):
    tm = cur.shape[0]
    k = w.shape[0]
    outs = []
    for s in range(cur.shape[1] // LANES):
        cs = slice(s * LANES, (s + 1) * LANES)
        gs = slice(col0 + s * LANES, col0 + (s + 1) * LANES)
        slab = cur[:, cs]
        win_ref[s, 0:SUBLANES, :] = tail_ref[:, gs]
        win_ref[s, SUBLANES:SUBLANES + tm, :] = slab
        tail_ref[:, gs] = slab[tm - SUBLANES:, :]
        out = slab * w[k - 1:k, cs] + b[:, cs]
        for d in range(1, k):
            out = out + win_ref[s, SUBLANES - d:SUBLANES - d + tm, :] * w[k - 1 - d:k - d, cs]
        outs.append(out)
    return jnp.concatenate(outs, axis=1)


def _qkv_kernel(x_ref, g_ref, w_ref, o_ref, *, d_model, q_scale):
    h = _rms_norm(x_ref[0], g_ref[...]).astype(BF16)
    for c in range(3):
        acc = _dot(h, w_ref[:, c * d_model:(c + 1) * d_model])
        if c == 0:
            acc = acc * q_scale
        o_ref[0, :, c * d_model:(c + 1) * d_model] = acc.astype(BF16)


def _qkv_proj(x, g, w):
    b, s, d = x.shape
    n = w.shape[1]
    tm = min(QKV_TILE, s)
    q_scale = (d // ATTN_HEADS) ** -0.5 * LOG2E
    return pl.pallas_call(
        functools.partial(_qkv_kernel, d_model=d, q_scale=q_scale),
        grid=(b, s // tm),
        in_specs=[
            pl.BlockSpec((1, tm, d), lambda i, j: (i, j, 0)),
            pl.BlockSpec((1, d), lambda i, j: (0, 0)),
            pl.BlockSpec((d, n), lambda i, j: (0, 0), pipeline_mode=pl.Buffered(1)),
        ],
        out_specs=pl.BlockSpec((1, tm, n), lambda i, j: (i, j, 0)),
        out_shape=jax.ShapeDtypeStruct((b, s, n), BF16),
        compiler_params=pltpu.CompilerParams(
            dimension_semantics=("parallel", "parallel"), vmem_limit_bytes=VMEM_LIMIT_BYTES),
        name="qkv_proj",
    )(x, g, w)


def _attn_kernel(q_ref, k_ref, v_ref, o_ref, acc_ref, car_ref, *, head_dim):
    qi = pl.program_id(2)
    tq = q_ref.shape[1]
    tk = tq // 2
    pairs = q_ref.shape[2] // LANES
    low_q = lax.broadcasted_iota(jnp.int32, (tq, LANES), 1) < head_dim
    low_k = lax.broadcasted_iota(jnp.int32, (tk, LANES), 1) < head_dim
    q_heads = []
    for p in range(pairs):
        qp = q_ref[0, :, p * LANES:(p + 1) * LANES]
        zero = jnp.zeros_like(qp)
        q_heads += [jnp.where(low_q, qp, zero), jnp.where(low_q, zero, qp)]
    row = lax.broadcasted_iota(jnp.int32, (tk, tk), 0)
    col = lax.broadcasted_iota(jnp.int32, (tk, tk), 1)
    suffix = jnp.where(row >= col, 1.0, 0.0).astype(BF16)
    causal = {tk: col < row,
              tq: lax.broadcasted_iota(jnp.int32, (tq, tk), 1) < lax.broadcasted_iota(jnp.int32, (tq, tk), 0)}
    acc_ref[...] = jnp.zeros_like(acc_ref)
    car_ref[...] = jnp.zeros_like(car_ref)

    def block(r0, nrow, j, masked):
        start = pl.multiple_of(j * tk, tk)
        rows = slice(r0, r0 + nrow)
        for p in range(pairs):
            k = k_ref[0, pl.ds(start, tk), p * LANES:(p + 1) * LANES]
            v = v_ref[0, pl.ds(start, tk), p * LANES:(p + 1) * LANES]
            zero = jnp.zeros_like(v)
            v2 = jnp.concatenate([jnp.where(low_k, v, zero), jnp.where(low_k, zero, v)], axis=0)
            ws = []
            for h in (2 * p, 2 * p + 1):
                z2 = lax.dot_general(q_heads[h][rows], k, (((1,), (1,)), ((), ())), preferred_element_type=F32)
                zb = z2.astype(BF16)
                pen = jnp.maximum(zb, 0.0) + jnp.log(1.0 + jnp.exp2(-jnp.abs(zb))) * LOG2E
                if masked:
                    pen = jnp.where(causal[nrow], pen, jnp.zeros_like(pen))
                csum = _dot(pen, suffix)
                car = car_ref[h, rows]
                w = jnp.exp2(z2 - csum - jnp.concatenate([car] * (tk // LANES), axis=1))
                if masked:
                    w = jnp.where(causal[nrow], w, 0.0)
                ws.append(w.astype(BF16))
                car_ref[h, rows] = car + jnp.broadcast_to(csum[:, 0:1], (nrow, LANES))
            acc_ref[p, rows] += _dot(jnp.concatenate(ws, axis=1), v2)

    block(tk, tk, 2 * qi + 1, True)
    block(0, tq, 2 * qi, True)

    def body(it, carry):
        block(0, tq, 2 * (qi - it) - 1, False)
        block(0, tq, 2 * (qi - it) - 2, False)
        return carry

    lax.fori_loop(0, qi, body, 0)
    for p in range(pairs):
        o_ref[0, :, p * LANES:(p + 1) * LANES] = acc_ref[p].astype(BF16)


def _attention(qkv, d_model):
    b, s, _ = qkv.shape
    head_dim = d_model // ATTN_HEADS
    assert 2 * head_dim == LANES
    tq = min(ATTN_Q_TILE, s)
    width = ATTN_HEAD_GROUP * head_dim
    groups = d_model // width
    pairs = width // LANES
    return pl.pallas_call(
        functools.partial(_attn_kernel, head_dim=head_dim),
        grid=(b, groups, s // tq),
        in_specs=[
            pl.BlockSpec((1, tq, width), lambda i, g, j: (i, j, g)),
            pl.BlockSpec((1, s, width), lambda i, g, j: (i, 0, groups + g)),
            pl.BlockSpec((1, s, width), lambda i, g, j: (i, 0, 2 * groups + g)),
        ],
        out_specs=pl.BlockSpec((1, tq, width), lambda i, g, j: (i, j, g)),
        out_shape=jax.ShapeDtypeStruct((b, s, d_model), BF16),
        scratch_shapes=[pltpu.VMEM((pairs, tq, LANES), F32), pltpu.VMEM((2 * pairs, tq, LANES), F32)],
        compiler_params=pltpu.CompilerParams(
            dimension_semantics=("parallel", "parallel", "arbitrary"), vmem_limit_bytes=VMEM_LIMIT_BYTES),
        name="stickbreak_attn",
    )(qkv, qkv, qkv)


def _rnn_kernel(x_ref, g_ref, win_ref, cw_ref, cb_ref, wg_ref, ba_ref, bx_ref, lru_ref, y_ref,
                rec_tail, h_tail, a_scr, u_scr, conv_scr, *, width):
    @pl.when(pl.program_id(1) == 0)
    def _():
        rec_tail[...] = jnp.zeros_like(rec_tail)
        h_tail[...] = jnp.zeros_like(h_tail)

    lru = lru_ref[...]
    log_sig = -(jnp.maximum(-lru, 0.0) + jnp.log1p(jnp.exp(-jnp.abs(lru))))
    rate = (LRU_C * LOG2E) * log_sig
    ts = conv_scr.shape[2] - SUBLANES
    for unit in range(x_ref.shape[1] // ts):
        _rnn_unit(x_ref, g_ref, win_ref, cw_ref, cb_ref, wg_ref, ba_ref, bx_ref, rate, y_ref, rec_tail, h_tail,
                  a_scr.at[unit], u_scr.at[unit], conv_scr.at[unit], unit * ts, ts, width)


def _rnn_unit(x_ref, g_ref, win_ref, cw_ref, cb_ref, wg_ref, ba_ref, bx_ref, rate, y_ref, rec_tail, h_tail,
              a_scr, u_scr, conv_scr, r0, ts, width):
    half = width // 2
    hn = _rms_norm(x_ref[0, r0:r0 + ts, :], g_ref[...]).astype(BF16)
    rec = _dot(hn, win_ref[:, width:])
    xr = _causal_conv(rec, rec_tail, 0, conv_scr, cw_ref[...], cb_ref[...])
    xrb = xr.astype(BF16)

    clen = ts // SUBLANES
    pitch = a_scr.shape[1] // SUBLANES
    slabs_per_half = half // LANES
    for hf in range(2):
        sl = slice(hf * half, (hf + 1) * half)
        pre = _dot(xrb[:, sl], wg_ref[hf])
        r = _sigmoid(pre[:, :half] + ba_ref[:, sl])
        i = _sigmoid(pre[:, half:] + bx_ref[:, sl])
        a = jnp.exp2(r * rate[:, sl])
        u = _sqrt_unit_interval(1.0 - a * a) * (i * xr[:, sl])
        for sh in range(slabs_per_half):
            s = hf * slabs_per_half + sh
            for c in range(SUBLANES):
                a_scr[s, c * pitch:c * pitch + clen, :] = a[c * clen:(c + 1) * clen, sh * LANES:(sh + 1) * LANES]
                u_scr[s, c * pitch:c * pitch + clen, :] = u[c * clen:(c + 1) * clen, sh * LANES:(sh + 1) * LANES]

    gate = _dot(hn, win_ref[:, :width])
    for s in range(width // LANES):
        cols = slice(s * LANES, (s + 1) * LANES)
        h = jnp.zeros((SUBLANES, LANES), F32)
        p = jnp.ones((SUBLANES, LANES), F32)
        for step in range(clen):
            idx = pl.ds(step, SUBLANES, stride=pitch)
            av = a_scr[s, idx, :]
            h = av * h + u_scr[s, idx, :]
            p = av * p
            u_scr[s, idx, :] = h
            a_scr[s, idx, :] = p
        enter = h_tail[:, cols]
        for c in range(SUBLANES):
            rows = slice(c * pitch, c * pitch + clen)
            h_full = u_scr[s, rows, :] + a_scr[s, rows, :] * enter
            out_rows = slice(r0 + c * clen, r0 + (c + 1) * clen)
            y_ref[0, out_rows, cols] = _gelu_times(gate[c * clen:(c + 1) * clen, cols], h_full).astype(BF16)
            enter = p[c:c + 1, :] * enter + h[c:c + 1, :]
        h_tail[:, cols] = enter


def _rnn_block(x, g, w_in, conv_w, conv_b, w_gates, b_a, b_x, lru):
    b, s, d = x.shape
    width = w_in.shape[1] // 2
    ts = min(RNN_TILE, s)
    sub = min(RNN_SUBTILE, ts)
    units = ts // sub
    scan_rows = SUBLANES * (sub // SUBLANES + SUBLANES // 2)
    const = lambda i, j: (0, 0)
    single = pl.Buffered(1)
    return pl.pallas_call(
        functools.partial(_rnn_kernel, width=width),
        grid=(b, s // ts),
        in_specs=[
            pl.BlockSpec((1, ts, d), lambda i, j: (i, j, 0)),
            pl.BlockSpec((1, d), const),
            pl.BlockSpec(w_in.shape, const, pipeline_mode=single),
            pl.BlockSpec(conv_w.shape, const),
            pl.BlockSpec((1, width), const),
            pl.BlockSpec(w_gates.shape, lambda i, j: (0, 0, 0), pipeline_mode=single),
            pl.BlockSpec((1, width), const),
            pl.BlockSpec((1, width), const),
            pl.BlockSpec((1, width), const),
        ],
        out_specs=pl.BlockSpec((1, ts, width), lambda i, j: (i, j, 0)),
        out_shape=jax.ShapeDtypeStruct((b, s, width), BF16),
        scratch_shapes=[
            pltpu.VMEM((SUBLANES, width), F32),
            pltpu.VMEM((1, width), F32),
            pltpu.VMEM((units, width // LANES, scan_rows, LANES), F32),
            pltpu.VMEM((units, width // LANES, scan_rows, LANES), F32),
            pltpu.VMEM((units, width // LANES, SUBLANES + sub, LANES), F32),
        ],
        compiler_params=pltpu.CompilerParams(
            dimension_semantics=("parallel", "arbitrary"), vmem_limit_bytes=VMEM_LIMIT_BYTES),
        name="rglru_block",
    )(x, g, w_in, conv_w, conv_b, w_gates, b_a, b_x, lru)


def _ffn_kernel(x_ref, m_ref, wm_ref, gf_ref, wup_ref, cw_ref, cb_ref, wdn_ref, gp_ref, wpg_ref, wpp_ref,
                p_ref, gfin_ref, o_ref, up_tail, win_ref, act_ref, *, ffn_dim, final_norm):
    @pl.when(pl.program_id(1) == 0)
    def _():
        up_tail[...] = jnp.zeros_like(up_tail)

    x1 = x_ref[0] + _dot(m_ref[0], wm_ref[...])
    h = _rms_norm(x1, gf_ref[...]).astype(BF16)
    for c in range(ffn_dim // FFN_CHUNK):
        parts = []
        for part, base in enumerate((0, ffn_dim)):
            col0 = base + c * FFN_CHUNK
            sl = slice(col0, col0 + FFN_CHUNK)
            up = _dot(h, wup_ref[:, sl])
            parts.append(_causal_conv(up, up_tail, col0, win_ref.at[c % 2, part], cw_ref[:, sl], cb_ref[:, sl]))
        act_ref[:, c * FFN_CHUNK:(c + 1) * FFN_CHUNK] = _gelu_times(parts[0], parts[1]).astype(BF16)
    x2 = x1 + _dot(act_ref[...], wdn_ref[...])
    hp = _rms_norm(x2, gp_ref[...]).astype(BF16)
    gate = _sigmoid(_dot(hp, wpg_ref[...]))
    x3 = x2 + gate * _dot(p_ref[0, 0].astype(BF16), wpp_ref[...])
    if final_norm:
        x3 = _rms_norm(x3, gfin_ref[...])
    o_ref[0] = x3


def _ffn_block(x, mix, w_mix, g_ffn, w_up, conv_w, conv_b, w_down, g_ple, w_pgate, w_pproj, p, layer,
               g_final, final_norm):
    b, s, d = x.shape
    km = mix.shape[2]
    ffn_dim = w_down.shape[0]
    ple_dim = p.shape[3]
    assert ffn_dim % FFN_CHUNK == 0
    tm = min(ROW_TILE, s)
    const = lambda i, j: (0, 0)
    single = pl.Buffered(1)
    return pl.pallas_call(
        functools.partial(_ffn_kernel, ffn_dim=ffn_dim, final_norm=final_norm),
        grid=(b, s // tm),
        in_specs=[
            pl.BlockSpec((1, tm, d), lambda i, j: (i, j, 0)),
            pl.BlockSpec((1, tm, km), lambda i, j: (i, j, 0)),
            pl.BlockSpec((km, d), const, pipeline_mode=single),
            pl.BlockSpec((1, d), const),
            pl.BlockSpec((d, 2 * ffn_dim), const, pipeline_mode=single),
            pl.BlockSpec(conv_w.shape, const),
            pl.BlockSpec((1, 2 * ffn_dim), const),
            pl.BlockSpec((ffn_dim, d), const, pipeline_mode=single),
            pl.BlockSpec((1, d), const),
            pl.BlockSpec((d, d), const, pipeline_mode=single),
            pl.BlockSpec((ple_dim, d), const, pipeline_mode=single),
            pl.BlockSpec((1, 1, tm, ple_dim), lambda i, j: (layer, i, j, 0)),
            pl.BlockSpec((1, d), const),
        ],
        out_specs=pl.BlockSpec((1, tm, d), lambda i, j: (i, j, 0)),
        out_shape=jax.ShapeDtypeStruct((b, s, d), F32),
        scratch_shapes=[
            pltpu.VMEM((SUBLANES, 2 * ffn_dim), F32),
            pltpu.VMEM((2, 2, FFN_CHUNK // LANES, SUBLANES + tm, LANES), F32),
            pltpu.VMEM((tm, ffn_dim), BF16),
        ],
        compiler_params=pltpu.CompilerParams(
            dimension_semantics=("parallel", "arbitrary"), vmem_limit_bytes=VMEM_LIMIT_BYTES),
        name="convffn_ple",
    )(x, mix, w_mix, g_ffn, w_up, conv_w, conv_b, w_down, g_ple, w_pgate, w_pproj, p, g_final)


def _block_diag_halves(w_a, w_x):
    heads, n, _ = w_a.shape
    hh = heads // 2
    eye = jnp.eye(hh, dtype=w_a.dtype)

    def dense(w):
        return jnp.einsum("hij,hg->higj", w, eye).reshape(hh * n, hh * n)

    halves = [jnp.concatenate([dense(w_a[hf * hh:(hf + 1) * hh]), dense(w_x[hf * hh:(hf + 1) * hh])], axis=1)
              for hf in range(2)]
    return jnp.stack(halves).astype(BF16)


def kernel(x, p, norm_mix, attn_w_qkv, attn_w_o, rnn_w_in, rnn_conv_w, rnn_conv_b, rnn_w_gate_a, rnn_b_gate_a, rnn_w_gate_x, rnn_b_gate_x, rnn_lru_param, rnn_w_out, norm_ffn, ffn_w_up, ffn_conv_w, ffn_conv_b, ffn_w_down, norm_ple, ple_w_gate, ple_w_proj, norm_final):
    depth = norm_mix.shape[0]
    d = x.shape[2]
    row = lambda v: v.reshape(1, -1)
    for i in range(depth):
        slot = i // 2
        if i % 2 == 0:
            qkv = _qkv_proj(x, row(norm_mix[i]), attn_w_qkv[slot].astype(BF16))
            mix = _attention(qkv, d)
            w_mix = attn_w_o[slot]
        else:
            mix = _rnn_block(
                x, row(norm_mix[i]), rnn_w_in[slot].astype(BF16), rnn_conv_w[slot], row(rnn_conv_b[slot]),
                _block_diag_halves(rnn_w_gate_a[slot], rnn_w_gate_x[slot]),
                row(rnn_b_gate_a[slot]), row(rnn_b_gate_x[slot]), row(rnn_lru_param[slot]))
            w_mix = rnn_w_out[slot]
        x = _ffn_block(
            x, mix, w_mix.astype(BF16), row(norm_ffn[i]), ffn_w_up[i].astype(BF16), ffn_conv_w[i],
            row(ffn_conv_b[i]), ffn_w_down[i].astype(BF16), row(norm_ple[i]), ple_w_gate[i].astype(BF16),
            ple_w_proj[i].astype(BF16), p, i, row(norm_final), final_norm=(i == depth - 1))
    return x
```

```python
import functools

import jax
import jax.numpy as jnp
from jax import lax
from jax.experimental import pallas as pl
from jax.experimental.pallas import tpu as pltpu

F32 = jnp.float32
BF16 = jnp.bfloat16

EPS = 1e-6
ATTN_HEADS = 16
RNN_HEADS = 16
LRU_C = 8.0

LANES = 128
SUBLANES = 8
MXU_DIM = 256
VMEM_LIMIT_BYTES = 56 * 1024 * 1024

ROW_TILE = 512
RNN_TILE = 1024
RNN_SUBTILE = 256
QKV_TILE = 1024
ATTN_Q_TILE = 2 * MXU_DIM
ATTN_HEAD_GROUP = 16
FFN_CHUNK = MXU_DIM
LOG2E = 1.4426950408889634
MIN_NORMAL = 1.1754943508222875e-38


def _rms_norm(x, g):
    ms = jnp.mean(x * x, axis=-1, keepdims=True)
    return (x * lax.rsqrt(ms + EPS)) * g


def _gelu_times(x, v):
    inner = x * (0.7978845608028654 + (0.7978845608028654 * 0.044715) * (x * x))
    return (x * v) * (0.5 + 0.5 * jnp.tanh(inner))


def _sigmoid(x):
    return 1.0 / (1.0 + jnp.exp2(x * (-LOG2E)))


def _sqrt_unit_interval(y):
    return y * lax.rsqrt(jnp.maximum(y, MIN_NORMAL))


def _dot(a, b):
    return jnp.dot(a, b, preferred_element_type=F32)


def _causal_conv(cur, tail_ref, col0, win_ref, w, b):
    tm = cur.shape[0]
    k = w.shape[0]
    outs = []
    for s in range(cur.shape[1] // LANES):
        cs = slice(s * LANES, (s + 1) * LANES)
        gs = slice(col0 + s * LANES, col0 + (s + 1) * LANES)
        slab = cur[:, cs]
        win_ref[s, 0:SUBLANES, :] = tail_ref[:, gs]
        win_ref[s, SUBLANES:SUBLANES + tm, :] = slab
        tail_ref[:, gs] = slab[tm - SUBLANES:, :]
        out = slab * w[k - 1:k, cs] + b[:, cs]
        for d in range(1, k):
            out = out + win_ref[s, SUBLANES - d:SUBLANES - d + tm, :] * w[k - 1 - d:k - d, cs]
        outs.append(out)
    return jnp.concatenate(outs, axis=1)


def _qkv_kernel(x_ref, g_ref, w_ref, o_ref, *, d_model, q_scale):
    h = _rms_norm(x_ref[0], g_ref[...]).astype(BF16)
    for c in range(3):
        acc = _dot(h, w_ref[:, c * d_model:(c + 1) * d_model])
        if c == 0:
            acc = acc * q_scale
        o_ref[0, :, c * d_model:(c + 1) * d_model] = acc.astype(BF16)


def _qkv_proj(x, g, w):
    b, s, d = x.shape
    n = w.shape[1]
    tm = min(QKV_TILE, s)
    q_scale = (d // ATTN_HEADS) ** -0.5 * LOG2E
    return pl.pallas_call(
        functools.partial(_qkv_kernel, d_model=d, q_scale=q_scale),
        grid=(b, s // tm),
        in_specs=[
            pl.BlockSpec((1, tm, d), lambda i, j: (i, j, 0)),
            pl.BlockSpec((1, d), lambda i, j: (0, 0)),
            pl.BlockSpec((d, n), lambda i, j: (0, 0), pipeline_mode=pl.Buffered(1)),
        ],
        out_specs=pl.BlockSpec((1, tm, n), lambda i, j: (i, j, 0)),
        out_shape=jax.ShapeDtypeStruct((b, s, n), BF16),
        compiler_params=pltpu.CompilerParams(
            dimension_semantics=("parallel", "parallel"), vmem_limit_bytes=VMEM_LIMIT_BYTES),
        name="qkv_proj",
    )(x, g, w)


def _attn_kernel(q_ref, k_ref, v_ref, o_ref, acc_ref, car_ref, *, head_dim):
    qi = pl.program_id(2)
    tq = q_ref.shape[1]
    tk = tq // 2
    pairs = q_ref.shape[2] // LANES
    low_q = lax.broadcasted_iota(jnp.int32, (tq, LANES), 1) < head_dim
    low_k = lax.broadcasted_iota(jnp.int32, (tk, LANES), 1) < head_dim
    q_heads = []
    for p in range(pairs):
        qp = q_ref[0, :, p * LANES:(p + 1) * LANES]
        zero = jnp.zeros_like(qp)
        q_heads += [jnp.where(low_q, qp, zero), jnp.where(low_q, zero, qp)]
    row = lax.broadcasted_iota(jnp.int32, (tk, tk), 0)
    col = lax.broadcasted_iota(jnp.int32, (tk, tk), 1)
    suffix = jnp.where(row >= col, 1.0, 0.0).astype(BF16)
    causal = {tk: col < row,
              tq: lax.broadcasted_iota(jnp.int32, (tq, tk), 1) < lax.broadcasted_iota(jnp.int32, (tq, tk), 0)}
    acc_ref[...] = jnp.zeros_like(acc_ref)
    car_ref[...] = jnp.zeros_like(car_ref)

    def block(r0, nrow, j, masked):
        start = pl.multiple_of(j * tk, tk)
        rows = slice(r0, r0 + nrow)
        for p in range(pairs):
            k = k_ref[0, pl.ds(start, tk), p * LANES:(p + 1) * LANES]
            v = v_ref[0, pl.ds(start, tk), p * LANES:(p + 1) * LANES]
            zero = jnp.zeros_like(v)
            v2 = jnp.concatenate([jnp.where(low_k, v, zero), jnp.where(low_k, zero, v)], axis=0)
            ws = []
            for h in (2 * p, 2 * p + 1):
                z2 = lax.dot_general(q_heads[h][rows], k, (((1,), (1,)), ((), ())), preferred_element_type=F32)
                zb = z2.astype(BF16)
                pen = jnp.maximum(zb, 0.0) + jnp.log(1.0 + jnp.exp2(-jnp.abs(zb))) * LOG2E
                if masked:
                    pen = jnp.where(causal[nrow], pen, jnp.zeros_like(pen))
                csum = _dot(pen, suffix)
                car = car_ref[h, rows]
                w = jnp.exp2(z2 - csum - jnp.concatenate([car] * (tk // LANES), axis=1))
                if masked:
                    w = jnp.where(causal[nrow], w, 0.0)
                ws.append(w.astype(BF16))
                car_ref[h, rows] = car + jnp.broadcast_to(csum[:, 0:1], (nrow, LANES))
            acc_ref[p, rows] += _dot(jnp.concatenate(ws, axis=1), v2)

    block(tk, tk, 2 * qi + 1, True)
    block(0, tq, 2 * qi, True)

    def body(it, carry):
        block(0, tq, 2 * (qi - it) - 1, False)
        block(0, tq, 2 * (qi - it) - 2, False)
        return carry

    lax.fori_loop(0, qi, body, 0)
    for p in range(pairs):
        o_ref[0, :, p * LANES:(p + 1) * LANES] = acc_ref[p].astype(BF16)


def _attention(qkv, d_model):
    b, s, _ = qkv.shape
    head_dim = d_model // ATTN_HEADS
    assert 2 * head_dim == LANES
    tq = min(ATTN_Q_TILE, s)
    width = ATTN_HEAD_GROUP * head_dim
    groups = d_model // width
    pairs = width // LANES
    return pl.pallas_call(
        functools.partial(_attn_kernel, head_dim=head_dim),
        grid=(b, groups, s // tq),
        in_specs=[
            pl.BlockSpec((1, tq, width), lambda i, g, j: (i, j, g)),
            pl.BlockSpec((1, s, width), lambda i, g, j: (i, 0, groups + g)),
            pl.BlockSpec((1, s, width), lambda i, g, j: (i, 0, 2 * groups + g)),
        ],
        out_specs=pl.BlockSpec((1, tq, width), lambda i, g, j: (i, j, g)),
        out_shape=jax.ShapeDtypeStruct((b, s, d_model), BF16),
        scratch_shapes=[pltpu.VMEM((pairs, tq, LANES), F32), pltpu.VMEM((2 * pairs, tq, LANES), F32)],
        compiler_params=pltpu.CompilerParams(
            dimension_semantics=("parallel", "parallel", "arbitrary"), vmem_limit_bytes=VMEM_LIMIT_BYTES),
        name="stickbreak_attn",
    )(qkv, qkv, qkv)


def _rnn_kernel(x_ref, g_ref, win_ref, cw_ref, cb_ref, wg_ref, ba_ref, bx_ref, lru_ref, y_ref,
                rec_tail, h_tail, a_scr, u_scr, conv_scr, *, width):
    @pl.when(pl.program_id(1) == 0)
    def _():
        rec_tail[...] = jnp.zeros_like(rec_tail)
        h_tail[...] = jnp.zeros_like(h_tail)

    lru = lru_ref[...]
    log_sig = -(jnp.maximum(-lru, 0.0) + jnp.log1p(jnp.exp(-jnp.abs(lru))))
    rate = (LRU_C * LOG2E) * log_sig
    ts = conv_scr.shape[2] - SUBLANES
    for unit in range(x_ref.shape[1] // ts):
        _rnn_unit(x_ref, g_ref, win_ref, cw_ref, cb_ref, wg_ref, ba_ref, bx_ref, rate, y_ref, rec_tail, h_tail,
                  a_scr.at[unit], u_scr.at[unit], conv_scr.at[unit], unit * ts, ts, width)


def _rnn_unit(x_ref, g_ref, win_ref, cw_ref, cb_ref, wg_ref, ba_ref, bx_ref, rate, y_ref, rec_tail, h_tail,
              a_scr, u_scr, conv_scr, r0, ts, width):
    half = width // 2
    hn = _rms_norm(x_ref[0, r0:r0 + ts, :], g_ref[...]).astype(BF16)
    rec = _dot(hn, win_ref[:, width:])
    xr = _causal_conv(rec, rec_tail, 0, conv_scr, cw_ref[...], cb_ref[...])
    xrb = xr.astype(BF16)

    clen = ts // SUBLANES
    pitch = a_scr.shape[1] // SUBLANES
    slabs_per_half = half // LANES
    for hf in range(2):
        sl = slice(hf * half, (hf + 1) * half)
        pre = _dot(xrb[:, sl], wg_ref[hf])
        r = _sigmoid(pre[:, :half] + ba_ref[:, sl])
        i = _sigmoid(pre[:, half:] + bx_ref[:, sl])
        a = jnp.exp2(r * rate[:, sl])
        u = _sqrt_unit_interval(1.0 - a * a) * (i * xr[:, sl])
        for sh in range(slabs_per_half):
            s = hf * slabs_per_half + sh
            for c in range(SUBLANES):
                a_scr[s, c * pitch:c * pitch + clen, :] = a[c * clen:(c + 1) * clen, sh * LANES:(sh + 1) * LANES]
                u_scr[s, c * pitch:c * pitch + clen, :] = u[c * clen:(c + 1) * clen, sh * LANES:(sh + 1) * LANES]

    gate = _dot(hn, win_ref[:, :width])
    for s in range(width // LANES):
        cols = slice(s * LANES, (s + 1) * LANES)
        h = jnp.zeros((SUBLANES, LANES), F32)
        p = jnp.ones((SUBLANES, LANES), F32)
        for step in range(clen):
            idx = pl.ds(step, SUBLANES, stride=pitch)
            av = a_scr[s, idx, :]
            h = av * h + u_scr[s, idx, :]
            p = av * p
            u_scr[s, idx, :] = h
            a_scr[s, idx, :] = p
        enter = h_tail[:, cols]
        for c in range(SUBLANES):
            rows = slice(c * pitch, c * pitch + clen)
            h_full = u_scr[s, rows, :] + a_scr[s, rows, :] * enter
            out_rows = slice(r0 + c * clen, r0 + (c + 1) * clen)
            y_ref[0, out_rows, cols] = _gelu_times(gate[c * clen:(c + 1) * clen, cols], h_full).astype(BF16)
            enter = p[c:c + 1, :] * enter + h[c:c + 1, :]
        h_tail[:, cols] = enter


def _rnn_block(x, g, w_in, conv_w, conv_b, w_gates, b_a, b_x, lru):
    b, s, d = x.shape
    width = w_in.shape[1] // 2
    ts = min(RNN_TILE, s)
    sub = min(RNN_SUBTILE, ts)
    units = ts // sub
    scan_rows = SUBLANES * (sub // SUBLANES + SUBLANES // 2)
    const = lambda i, j: (0, 0)
    single = pl.Buffered(1)
    return pl.pallas_call(
        functools.partial(_rnn_kernel, width=width),
        grid=(b, s // ts),
        in_specs=[
            pl.BlockSpec((1, ts, d), lambda i, j: (i, j, 0)),
            pl.BlockSpec((1, d), const),
            pl.BlockSpec(w_in.shape, const, pipeline_mode=single),
            pl.BlockSpec(conv_w.shape, const),
            pl.BlockSpec((1, width), const),
            pl.BlockSpec(w_gates.shape, lambda i, j: (0, 0, 0), pipeline_mode=single),
            pl.BlockSpec((1, width), const),
            pl.BlockSpec((1, width), const),
            pl.BlockSpec((1, width), const),
        ],
        out_specs=pl.BlockSpec((1, ts, width), lambda i, j: (i, j, 0)),
        out_shape=jax.ShapeDtypeStruct((b, s, width), BF16),
        scratch_shapes=[
            pltpu.VMEM((SUBLANES, width), F32),
            pltpu.VMEM((1, width), F32),
            pltpu.VMEM((units, width // LANES, scan_rows, LANES), F32),
            pltpu.VMEM((units, width // LANES, scan_rows, LANES), F32),
            pltpu.VMEM((units, width // LANES, SUBLANES + sub, LANES), F32),
        ],
        compiler_params=pltpu.CompilerParams(
            dimension_semantics=("parallel", "arbitrary"), vmem_limit_bytes=VMEM_LIMIT_BYTES),
        name="rglru_block",
    )(x, g, w_in, conv_w, conv_b, w_gates, b_a, b_x, lru)


def _ffn_kernel(x_ref, m_ref, wm_ref, gf_ref, wup_ref, cw_ref, cb_ref, wdn_ref, gp_ref, wpg_ref, wpp_ref,
                p_ref, gfin_ref, o_ref, up_tail, win_ref, act_ref, *, ffn_dim, final_norm):
    @pl.when(pl.program_id(1) == 0)
    def _():
        up_tail[...] = jnp.zeros_like(up_tail)

    x1 = x_ref[0] + _dot(m_ref[0], wm_ref[...])
    h = _rms_norm(x1, gf_ref[...]).astype(BF16)
    for c in range(ffn_dim // FFN_CHUNK):
        parts = []
        for part, base in enumerate((0, ffn_dim)):
            col0 = base + c * FFN_CHUNK
            sl = slice(col0, col0 + FFN_CHUNK)
            up = _dot(h, wup_ref[:, sl])
            parts.append(_causal_conv(up, up_tail, col0, win_ref.at[c % 2, part], cw_ref[:, sl], cb_ref[:, sl]))
        act_ref[:, c * FFN_CHUNK:(c + 1) * FFN_CHUNK] = _gelu_times(parts[0], parts[1]).astype(BF16)
    x2 = x1 + _dot(act_ref[...], wdn_ref[...])
    hp = _rms_norm(x2, gp_ref[...]).astype(BF16)
    gate = _sigmoid(_dot(hp, wpg_ref[...]))
    x3 = x2 + gate * _dot(p_ref[0, 0].astype(BF16), wpp_ref[...])
    if final_norm:
        x3 = _rms_norm(x3, gfin_ref[...])
    o_ref[0] = x3


def _ffn_block(x, mix, w_mix, g_ffn, w_up, conv_w, conv_b, w_down, g_ple, w_pgate, w_pproj, p, layer,
               g_final, final_norm):
    b, s, d = x.shape
    km = mix.shape[2]
    ffn_dim = w_down.shape[0]
    ple_dim = p.shape[3]
    assert ffn_dim % FFN_CHUNK == 0
    tm = min(ROW_TILE, s)
    const = lambda i, j: (0, 0)
    single = pl.Buffered(1)
    return pl.pallas_call(
        functools.partial(_ffn_kernel, ffn_dim=ffn_dim, final_norm=final_norm),
        grid=(b, s // tm),
        in_specs=[
            pl.BlockSpec((1, tm, d), lambda i, j: (i, j, 0)),
            pl.BlockSpec((1, tm, km), lambda i, j: (i, j, 0)),
            pl.BlockSpec((km, d), const, pipeline_mode=single),
            pl.BlockSpec((1, d), const),
            pl.BlockSpec((d, 2 * ffn_dim), const, pipeline_mode=single),
            pl.BlockSpec(conv_w.shape, const),
            pl.BlockSpec((1, 2 * ffn_dim), const),
            pl.BlockSpec((ffn_dim, d), const, pipeline_mode=single),
            pl.BlockSpec((1, d), const),
            pl.BlockSpec((d, d), const, pipeline_mode=single),
            pl.BlockSpec((ple_dim, d), const, pipeline_mode=single),
            pl.BlockSpec((1, 1, tm, ple_dim), lambda i, j: (layer, i, j, 0)),
            pl.BlockSpec((1, d), const),
        ],
        out_specs=pl.BlockSpec((1, tm, d), lambda i, j: (i, j, 0)),
        out_shape=jax.ShapeDtypeStruct((b, s, d), F32),
        scratch_shapes=[
            pltpu.VMEM((SUBLANES, 2 * ffn_dim), F32),
            pltpu.VMEM((2, 2, FFN_CHUNK // LANES, SUBLANES + tm, LANES), F32),
            pltpu.VMEM((tm, ffn_dim), BF16),
        ],
        compiler_params=pltpu.CompilerParams(
            dimension_semantics=("parallel", "arbitrary"), vmem_limit_bytes=VMEM_LIMIT_BYTES),
        name="convffn_ple",
    )(x, mix, w_mix, g_ffn, w_up, conv_w, conv_b, w_down, g_ple, w_pgate, w_pproj, p, g_final)


def _block_diag_halves(w_a, w_x):
    heads, n, _ = w_a.shape
    hh = heads // 2
    eye = jnp.eye(hh, dtype=w_a.dtype)

    def dense(w):
        return jnp.einsum("hij,hg->higj", w, eye).reshape(hh * n, hh * n)

    halves = [jnp.concatenate([dense(w_a[hf * hh:(hf + 1) * hh]), dense(w_x[hf * hh:(hf + 1) * hh])], axis=1)
              for hf in range(2)]
    return jnp.stack(halves).astype(BF16)


def kernel(x, p, norm_mix, attn_w_qkv, attn_w_o, rnn_w_in, rnn_conv_w, rnn_conv_b, rnn_w_gate_a, rnn_b_gate_a, rnn_w_gate_x, rnn_b_gate_x, rnn_lru_param, rnn_w_out, norm_ffn, ffn_w_up, ffn_conv_w, ffn_conv_b, ffn_w_down, norm_ple, ple_w_gate, ple_w_proj, norm_final):
    depth = norm_mix.shape[0]
    d = x.shape[2]
    row = lambda v: v.reshape(1, -1)
    for i in range(depth):
        slot = i // 2
        if i % 2 == 0:
            qkv = _qkv_proj(x, row(norm_mix[i]), attn_w_qkv[slot].astype(BF16))
            mix = _attention(qkv, d)
            w_mix = attn_w_o[slot]
        else:
            mix = _rnn_block(
                x, row(norm_mix[i]), rnn_w_in[slot].astype(BF16), rnn_conv_w[slot], row(rnn_conv_b[slot]),
                _block_diag_halves(rnn_w_gate_a[slot], rnn_w_gate_x[slot]),
                row(rnn_b_gate_a[slot]), row(rnn_b_gate_x[slot]), row(rnn_lru_param[slot]))
            w_mix = rnn_w_out[slot]
        x = _ffn_block(
            x, mix, w_mix.astype(BF16), row(norm_ffn[i]), ffn_w_up[i].astype(BF16), ffn_conv_w[i],
            row(ffn_conv_b[i]), ffn_w_down[i].astype(BF16), row(norm_ple[i]), ple_w_gate[i].astype(BF16),
            ple_w_proj[i].astype(BF16), p, i, row(norm_final), final_norm=(i == depth - 1))
    return x
```
